```python
import math
import jax, jax.numpy as jnp
from jax import lax
import numpy as np

D_MODEL = 2048
BATCH = 8
SEQ = 4096
DEPTH = 4

CTX_LEN = 256
GRID_W = 64
N_MIXERS = 2
NORM_EPS = 1e-6
CONV_W = 5

M_D_INNER = 2 * D_MODEL
M_HEADDIM = 64
M_HEADS = M_D_INNER // M_HEADDIM
M_D_STATE = 128
M_GROUPS = 8
M_BC_DIM = M_GROUPS * M_D_STATE
M_CONV_DIM = M_D_INNER + 2 * M_BC_DIM
M_IN_DIM = M_D_INNER + M_CONV_DIM + 2 * M_HEADS
SSD_CHUNK = 128

L_HEADS = 8
L_QK_DIM = D_MODEL // 2 // L_HEADS
L_V_DIM = D_MODEL // L_HEADS
L_QK_TOT = L_HEADS * L_QK_DIM
L_V_TOT = L_HEADS * L_V_DIM
L_QKV_DIM = 2 * L_QK_TOT + L_V_TOT
L_IN_DIM = L_QKV_DIM + L_V_TOT + 4 * L_HEADS
MLSTM_CHUNK = 64

N_EXPERTS = 32
TOP_K = 4
D_EXPERT = 3 * D_MODEL // 8
SWIGLU_LIMIT = 7.0
SWIGLU_ALPHA = 1.702
MOE_BLOCK = 256

kernel_name = 'hybrid_ssd_mlstm_moe_prefix_dit'


def rms_norm(x, g):
    xf = x.astype(jnp.float32)
    y = xf * lax.rsqrt(jnp.mean(xf * xf, axis=-1, keepdims=True) + NORM_EPS)
    return (y * g.astype(jnp.float32)).astype(x.dtype)


def group_rms(y, g, n_groups):
    shp = y.shape
    yg = y.reshape(shp[:-1] + (n_groups, shp[-1] // n_groups))
    yg = yg * lax.rsqrt(jnp.mean(yg * yg, axis=-1, keepdims=True) + NORM_EPS)
    return yg.reshape(shp) * g.astype(jnp.float32)


def modulate(x, shift, scale):
    return x * (1 + scale) + shift


def depthwise_conv(u, w, b):
    y = lax.conv_general_dilated(u, w[:, None, :], window_strides=(1,),
                                 padding=[(CONV_W // 2, CONV_W // 2)],
                                 dimension_numbers=('NWC', 'WIO', 'NWC'),
                                 feature_group_count=u.shape[-1])
    return jax.nn.silu(y + b)


def to_chunks(u, chunk):
    b, t = u.shape[:2]
    return jnp.moveaxis(u.reshape((b, t // chunk, chunk) + u.shape[2:]), 1, 0)


def from_chunks(u):
    u = jnp.moveaxis(u, 0, 1)
    return u.reshape((u.shape[0], u.shape[1] * u.shape[2]) + u.shape[3:])


def flip_if(u, reverse):
    return jnp.flip(u, 1) if reverse else u


def ssd_scan(x, dt, a, bm, cm, h0):
    bsz, t, nh, p = x.shape
    g, n = bm.shape[2], bm.shape[3]
    r = nh // g
    xdt = to_chunks((x.astype(jnp.float32) * dt[..., None]).reshape(bsz, t, g, r, p), SSD_CHUNK)
    adt = to_chunks((dt * a).reshape(bsz, t, g, r), SSD_CHUNK)
    bc = to_chunks(bm.astype(jnp.float32), SSD_CHUNK)
    cc = to_chunks(cm.astype(jnp.float32), SSD_CHUNK)
    mask = jnp.tril(jnp.ones((SSD_CHUNK, SSD_CHUNK), bool))[None, :, :, None, None]

    def step(h, inp):
        xk, ak, bk, ck = inp
        cum = jnp.cumsum(ak, axis=1)
        decay = jnp.exp(jnp.where(mask, cum[:, :, None] - cum[:, None], -jnp.inf))
        w = jnp.einsum('blgn,bsgn->blsg', ck, bk)[..., None] * decay
        y = (jnp.einsum('blsgr,bsgrp->blgrp', w, xk)
             + jnp.einsum('blgn,bgrpn->blgrp', ck, h) * jnp.exp(cum)[..., None])
        xw = xk * jnp.exp(cum[:, -1:] - cum)[..., None]
        h = h * jnp.exp(cum[:, -1])[..., None, None] + jnp.einsum('bsgn,bsgrp->bgrpn', bk, xw)
        return h, y

    h, ys = lax.scan(step, h0.reshape(bsz, g, r, p, n), (xdt, adt, bc, cc))
    return from_chunks(ys).reshape(bsz, t, nh, p), h.reshape(bsz, nh, p, n)


def mlstm_scan(q, k, v, i_pre, f_pre, state):
    qs = to_chunks(q.astype(jnp.float32) * L_QK_DIM ** -0.5, MLSTM_CHUNK)
    ks = to_chunks(k.astype(jnp.float32), MLSTM_CHUNK)
    vs = to_chunks(v.astype(jnp.float32), MLSTM_CHUNK)
    ig = to_chunks(i_pre, MLSTM_CHUNK)
    fg = to_chunks(f_pre, MLSTM_CHUNK)
    mask = jnp.tril(jnp.ones((MLSTM_CHUNK, MLSTM_CHUNK), bool))[None, :, :, None]

    def step(carry, inp):
        c_mat, n_vec, m = carry
        qk, kk, vk, ik, fk = inp
        b = jnp.cumsum(jax.nn.log_sigmoid(fk), axis=1)
        dmat = jnp.where(mask, b[:, :, None] - b[:, None] + ik[:, None], -jnp.inf)
        inter = b + m[:, None]
        m_out = jnp.maximum(inter, dmat.max(axis=2))
        w = jnp.exp(dmat - m_out[:, :, None]) * jnp.einsum('blhd,bshd->blsh', qk, kk)
        inter_w = jnp.exp(inter - m_out)
        num = (jnp.einsum('blsh,bshv->blhv', w, vk)
               + inter_w[..., None] * jnp.einsum('blhd,bhdv->blhv', qk, c_mat))
        den = w.sum(axis=2) + inter_w * jnp.einsum('blhd,bhd->blh', qk, n_vec)
        hk = num / jnp.maximum(jnp.abs(den), jnp.exp(-m_out))[..., None]
        b_end = b[:, -1]
        to_end = b_end[:, None] - b + ik
        m_new = jnp.maximum(b_end + m, to_end.max(axis=1))
        carry_w = jnp.exp(b_end + m - m_new)
        kw = kk * jnp.exp(to_end - m_new[:, None])[..., None]
        c_mat = carry_w[..., None, None] * c_mat + jnp.einsum('bshd,bshv->bhdv', kw, vk)
        n_vec = carry_w[..., None] * n_vec + kw.sum(axis=1)
        return (c_mat, n_vec, m_new), hk

    state, hs = lax.scan(step, state, (qs, ks, vs, ig, fg))
    return from_chunks(hs), state


def mamba_mixer(h, hc, in_w, conv_w, conv_b, dt_bias, a_log, d_skip, norm_g, out_w, ctx_out):
    def project(u):
        bsz, t = u.shape[:2]
        zxbcdt = u @ in_w
        z = zxbcdt[..., :M_D_INNER]
        xbc = depthwise_conv(zxbcdt[..., M_D_INNER:M_D_INNER + M_CONV_DIM], conv_w, conv_b)
        xs = xbc[..., :M_D_INNER].reshape(bsz, t, M_HEADS, M_HEADDIM)
        bm = xbc[..., M_D_INNER:M_D_INNER + M_BC_DIM].reshape(bsz, t, M_GROUPS, M_D_STATE)
        cm = xbc[..., M_D_INNER + M_BC_DIM:].reshape(bsz, t, M_GROUPS, M_D_STATE)
        dt = jax.nn.softplus(zxbcdt[..., M_D_INNER + M_CONV_DIM:].astype(jnp.float32)
                             .reshape(bsz, t, 2, M_HEADS) + dt_bias.astype(jnp.float32))
        return z, xs, bm, cm, dt

    zl, xl, bl, cl, dtl = project(h)
    zc, xc, bc, cc, dtc = project(hc)
    a = -jnp.exp(a_log.astype(jnp.float32))
    h0 = jnp.zeros((h.shape[0], M_HEADS, M_HEADDIM, M_D_STATE), jnp.float32)

    def direction(d, rev):
        yc, sc = ssd_scan(flip_if(xc, rev), flip_if(dtc[:, :, d], rev), a[d],
                          flip_if(bc, rev), flip_if(cc, rev), h0)
        yl, _ = ssd_scan(flip_if(xl, rev), flip_if(dtl[:, :, d], rev), a[d],
                         flip_if(bl, rev), flip_if(cl, rev), sc)
        return flip_if(yc, rev), flip_if(yl, rev)

    yc_f, yl_f = direction(0, False)
    yc_b, yl_b = direction(1, True)

    def finish(y_f, y_b, xs, z):
        bsz, t = z.shape[:2]
        y = y_f + y_b + d_skip.astype(jnp.float32)[:, None] * xs.astype(jnp.float32)
        y = y.reshape(bsz, t, M_D_INNER) * jax.nn.silu(z.astype(jnp.float32))
        return group_rms(y, norm_g, M_GROUPS).astype(z.dtype) @ out_w

    y_ctx = finish(yc_f, yc_b, xc, zc) if ctx_out else None
    return finish(yl_f, yl_b, xl, zl), y_ctx


def mlstm_mixer(h, hc, in_w, conv_w, conv_b, gate_b, norm_g, out_w, ctx_out):
    def project(u):
        bsz, t = u.shape[:2]
        proj = u @ in_w
        qkv = depthwise_conv(proj[..., :L_QKV_DIM], conv_w, conv_b)
        q = qkv[..., :L_QK_TOT].reshape(bsz, t, L_HEADS, L_QK_DIM)
        k = qkv[..., L_QK_TOT:2 * L_QK_TOT].reshape(bsz, t, L_HEADS, L_QK_DIM)
        v = qkv[..., 2 * L_QK_TOT:].reshape(bsz, t, L_HEADS, L_V_DIM)
        o = proj[..., L_QKV_DIM:L_QKV_DIM + L_V_TOT]
        gates = (proj[..., L_QKV_DIM + L_V_TOT:].astype(jnp.float32)
                 .reshape(bsz, t, 2, 2, L_HEADS) + gate_b.astype(jnp.float32))
        return q, k, v, o, gates

    ql, kl, vl, ol, gl = project(h)
    qc, kc, vc, oc, gc = project(hc)
    bsz = h.shape[0]
    state0 = (jnp.zeros((bsz, L_HEADS, L_QK_DIM, L_V_DIM), jnp.float32),
              jnp.zeros((bsz, L_HEADS, L_QK_DIM), jnp.float32),
              jnp.full((bsz, L_HEADS), -jnp.inf, jnp.float32))

    def direction(d, rev):
        hc_d, st = mlstm_scan(flip_if(qc, rev), flip_if(kc, rev), flip_if(vc, rev),
                              flip_if(gc[:, :, d, 0], rev), flip_if(gc[:, :, d, 1], rev), state0)
        hl_d, _ = mlstm_scan(flip_if(ql, rev), flip_if(kl, rev), flip_if(vl, rev),
                             flip_if(gl[:, :, d, 0], rev), flip_if(gl[:, :, d, 1], rev), st)
        return flip_if(hc_d, rev), flip_if(hl_d, rev)

    hc_f, hl_f = direction(0, False)
    hc_b, hl_b = direction(1, True)

    def finish(h_f, h_b, o):
        bsz_, t = o.shape[:2]
        y = group_rms((h_f + h_b).reshape(bsz_, t, L_V_TOT), norm_g, L_HEADS)
        y = y * jax.nn.sigmoid(o.astype(jnp.float32))
        return y.astype(o.dtype) @ out_w

    y_ctx = finish(hc_f, hc_b, oc) if ctx_out else None
    return finish(hl_f, hl_b, ol), y_ctx


def moe_ffn(u, router_w, router_b, gu_w, gu_b, down_w, down_b):
    n_tok = u.shape[0]
    logits = (u @ router_w).astype(jnp.float32) + router_b.astype(jnp.float32)
    top_logit, top_idx = lax.top_k(logits, TOP_K)
    gates = jax.nn.softmax(top_logit, axis=-1).astype(u.dtype)
    n_assign = n_tok * TOP_K
    flat_e = top_idx.reshape(-1)
    order = jnp.argsort(flat_e)
    sorted_e = flat_e[order]
    counts = jnp.bincount(flat_e, length=N_EXPERTS)
    padded = (counts + MOE_BLOCK - 1) // MOE_BLOCK * MOE_BLOCK
    start = jnp.cumsum(counts) - counts
    pad_end = jnp.cumsum(padded)
    pad_start = pad_end - padded
    slot = jnp.arange(n_assign) - start[sorted_e] + pad_start[sorted_e]
    n_blocks = -(-n_assign // MOE_BLOCK) + N_EXPERTS
    n_slots = n_blocks * MOE_BLOCK
    slot_tok = jnp.zeros((n_slots,), jnp.int32).at[slot].set((order // TOP_K).astype(jnp.int32))
    slot_gate = jnp.zeros((n_slots,), u.dtype).at[slot].set(gates.reshape(-1)[order])
    block_e = jnp.minimum(jnp.searchsorted(pad_end, jnp.arange(n_blocks) * MOE_BLOCK, side='right'),
                          N_EXPERTS - 1)

    def expert_block(acc, inp):
        tok, g, e = inp
        hgu = u[tok] @ gu_w[e] + gu_b[e]
        glu = jnp.minimum(hgu[:, :D_EXPERT], SWIGLU_LIMIT)
        lin = jnp.clip(hgu[:, D_EXPERT:], -SWIGLU_LIMIT, SWIGLU_LIMIT)
        act = glu * jax.nn.sigmoid(SWIGLU_ALPHA * glu) * (lin + 1)
        yb = (act @ down_w[e] + down_b[e]) * g[:, None]
        return acc.at[tok].add(yb), None

    out, _ = lax.scan(expert_block, jnp.zeros_like(u),
                      (slot_tok.reshape(n_blocks, MOE_BLOCK), slot_gate.reshape(n_blocks, MOE_BLOCK), block_e))
    return out


def setup_inputs(seed: int = 0) -> dict:
    key = jax.random.key(seed)
    ks = iter(jax.random.split(key, 40))
    d = D_MODEL
    n_mamba = (DEPTH + 1) // 2
    n_mlstm = DEPTH // 2

    def nrm(shape, std):
        return jax.random.normal(next(ks), shape, jnp.float32) * std

    def gain(shape):
        return 1.0 + nrm(shape, 0.02)

    dt0 = jnp.exp(jax.random.uniform(next(ks), (n_mamba, 2, M_HEADS), jnp.float32,
                                     math.log(1e-3), math.log(1e-1)))
    a0 = jax.random.uniform(next(ks), (n_mamba, 2, M_HEADS), jnp.float32, 1.0, 16.0)
    i_bias = nrm((n_mlstm, 2, L_HEADS), 0.1)
    f_bias = 3.0 + jax.random.uniform(next(ks), (n_mlstm, 2, L_HEADS), jnp.float32, 0.0, 3.0)
    return {
        'x': nrm((BATCH, SEQ, d), 1.0),
        'c': nrm((BATCH, d), 1.0),
        'ctx': nrm((BATCH, CTX_LEN, d), 1.0),
        'c_ctx': nrm((d,), 1.0),
        'mod_w': nrm((DEPTH, d, 6 * d), 0.5 * d ** -0.5),
        'mod_b': nrm((DEPTH, 6 * d), 0.02),
        'norm1_g': gain((DEPTH, d)),
        'norm2_g': gain((DEPTH, d)),
        'final_g': gain((d,)),
        'm_in_w': nrm((n_mamba, d, M_IN_DIM), d ** -0.5),
        'm_conv_w': nrm((n_mamba, CONV_W, M_CONV_DIM), CONV_W ** -0.5),
        'm_conv_b': nrm((n_mamba, M_CONV_DIM), 0.02),
        'm_dt_bias': dt0 + jnp.log(-jnp.expm1(-dt0)),
        'm_a_log': jnp.log(a0),
        'm_d': gain((n_mamba, M_HEADS)),
        'm_norm_g': gain((n_mamba, M_D_INNER)),
        'm_out_w': nrm((n_mamba, M_D_INNER, d), M_D_INNER ** -0.5),
        'l_in_w': nrm((n_mlstm, d, L_IN_DIM), d ** -0.5),
        'l_conv_w': nrm((n_mlstm, CONV_W, L_QKV_DIM), CONV_W ** -0.5),
        'l_conv_b': nrm((n_mlstm, L_QKV_DIM), 0.02),
        'l_gate_b': jnp.stack([i_bias, f_bias], axis=2),
        'l_norm_g': gain((n_mlstm, L_V_TOT)),
        'l_out_w': nrm((n_mlstm, L_V_TOT, d), L_V_TOT ** -0.5),
        'router_w': nrm((DEPTH, d, N_EXPERTS), d ** -0.5),
        'router_b': nrm((DEPTH, N_EXPERTS), 0.01),
        'e_gu_w': nrm((DEPTH, N_EXPERTS, d, 2 * D_EXPERT), d ** -0.5),
        'e_gu_b': nrm((DEPTH, N_EXPERTS, 2 * D_EXPERT), 0.02),
        'e_down_w': nrm((DEPTH, N_EXPERTS, D_EXPERT, d), D_EXPERT ** -0.5),
        'e_down_b': nrm((DEPTH, N_EXPERTS, d), 0.02),
    }


def reference(x, c, ctx, c_ctx, mod_w, mod_b, norm1_g, norm2_g, final_g,
              m_in_w, m_conv_w, m_conv_b, m_dt_bias, m_a_log, m_d, m_norm_g, m_out_w,
              l_in_w, l_conv_w, l_conv_b, l_gate_b, l_norm_g, l_out_w,
              router_w, router_b, e_gu_w, e_gu_b, e_down_w, e_down_b):
    bsz, seq, d = x.shape
    rows = seq // GRID_W

    def to_cols(u):
        return u.reshape(bsz, rows, GRID_W, d).swapaxes(1, 2).reshape(bsz, seq, d)

    def from_cols(u):
        return u.reshape(bsz, GRID_W, rows, d).swapaxes(1, 2).reshape(bsz, seq, d)

    cs = jax.nn.silu(c)
    cs_ctx = jax.nn.silu(c_ctx)
    xc = ctx
    n_lat = bsz * seq
    for i in range(DEPTH):
        last = i == DEPTH - 1
        j = i // N_MIXERS
        mod = (cs @ mod_w[i] + mod_b[i])[:, None, :]
        mod_c = cs_ctx @ mod_w[i] + mod_b[i]
        sh1, sc1, g1, sh2, sc2, g2 = jnp.split(mod, 6, axis=-1)
        sh1c, sc1c, g1c, sh2c, sc2c, g2c = jnp.split(mod_c, 6, axis=-1)

        h = modulate(rms_norm(x, norm1_g[i]), sh1, sc1)
        hc = modulate(rms_norm(xc, norm1_g[i]), sh1c, sc1c)
        if i % N_MIXERS == 0:
            y, yc = mamba_mixer(h, hc, m_in_w[j], m_conv_w[j], m_conv_b[j], m_dt_bias[j], m_a_log[j],
                                m_d[j], m_norm_g[j], m_out_w[j], not last)
        else:
            y, yc = mlstm_mixer(to_cols(h), hc, l_in_w[j], l_conv_w[j], l_conv_b[j], l_gate_b[j],
                                l_norm_g[j], l_out_w[j], not last)
            y = from_cols(y)
        x = x + g1 * y

        u = modulate(rms_norm(x, norm2_g[i]), sh2, sc2).reshape(n_lat, d)
        if last:
            f = moe_ffn(u, router_w[i], router_b[i], e_gu_w[i], e_gu_b[i], e_down_w[i], e_down_b[i])
            x = x + g2 * f.reshape(bsz, seq, d)
        else:
            xc = xc + g1c * yc
            uc = modulate(rms_norm(xc, norm2_g[i]), sh2c, sc2c).reshape(-1, d)
            f = moe_ffn(jnp.concatenate([u, uc], axis=0), router_w[i], router_b[i],
                        e_gu_w[i], e_gu_b[i], e_down_w[i], e_down_b[i])
            x = x + g2 * f[:n_lat].reshape(bsz, seq, d)
            xc = xc + g2c * f[n_lat:].reshape(xc.shape)
    return rms_norm(x, final_g)
```

```python
import functools

import jax
import jax.numpy as jnp
from jax import lax
from jax.experimental import pallas as pl
from jax.experimental.pallas import tpu as pltpu

f32 = jnp.float32
bf16 = jnp.bfloat16

NORM_EPS = 1e-6
CONV_W = 5
GRID_W = 64
M_D_STATE = 128
TOP_K = 4
SWIGLU_LIMIT = 7.0
SWIGLU_ALPHA = 1.702
SSD_CHUNK = 128
MLSTM_CHUNK = 256
ROW_TILE = 256
LANE = 128
V7X_VMEM_LIMIT = 60000 * 1024


def _cparams(sem):
    return pltpu.CompilerParams(dimension_semantics=sem, vmem_limit_bytes=V7X_VMEM_LIMIT)


def _normmod_kernel(xl_ref, xc_ref, g_ref, mod_ref, ws_ref, h_ref, small_ref, *, n_ctx, cols, d):
    j = pl.program_id(1)

    def rows(x, shift, scale, r0, nr):
        ms = jnp.mean(x * x, axis=-1, keepdims=True)
        y = (x * lax.rsqrt(ms + NORM_EPS)) * g_ref[...]
        hb = (y * (1.0 + scale) + shift).astype(bf16)
        h_ref[0, r0:r0 + nr, :] = hb
        small_ref[0, r0:r0 + nr, :] = jnp.dot(hb, ws_ref[...], preferred_element_type=f32)

    @pl.when(j < n_ctx)
    def _():
        rows(xc_ref[0], mod_ref[0, 0:1, :], mod_ref[0, 1:2, :], 0, xc_ref.shape[1])

    @pl.when(j >= n_ctx)
    def _():
        if cols == 0:
            rows(xl_ref[0], mod_ref[0, 2:3, :], mod_ref[0, 3:4, :], 0, xl_ref.shape[1])
        else:
            nr = xl_ref.shape[1]
            for ci in range(cols):
                rows(xl_ref[0, :, ci * d:(ci + 1) * d], mod_ref[0, 2:3, :], mod_ref[0, 3:4, :], ci * nr, nr)


def normmod(x_lat, x_ctx, g, mods, w_small, col_major):
    b, t, d = x_lat.shape
    ctx = x_ctx.shape[1]
    tt = ROW_TILE
    n_ctx, n_lat = ctx // tt, t // tt
    assert ctx % tt == 0 and t % tt == 0
    if col_major:
        r = t // GRID_W
        cols = tt // r
        assert tt % r == 0 and GRID_W % cols == 0
        xl = x_lat.reshape(b, r, GRID_W * d)
        xl_spec = pl.BlockSpec((1, r, cols * d), lambda bi, j: (bi, 0, jnp.maximum(j - n_ctx, 0)))
    else:
        cols = 0
        xl = x_lat
        xl_spec = pl.BlockSpec((1, tt, d), lambda bi, j: (bi, jnp.maximum(j - n_ctx, 0), 0))
    s = ctx + t
    return pl.pallas_call(
        functools.partial(_normmod_kernel, n_ctx=n_ctx, cols=cols, d=d),
        grid=(b, n_ctx + n_lat),
        in_specs=[
            xl_spec,
            pl.BlockSpec((1, tt, d), lambda bi, j: (bi, jnp.minimum(j, n_ctx - 1), 0)),
            pl.BlockSpec((1, d), lambda bi, j: (0, 0)),
            pl.BlockSpec((1, 4, d), lambda bi, j: (bi, 0, 0)),
            pl.BlockSpec((d, LANE), lambda bi, j: (0, 0)),
        ],
        out_specs=[
            pl.BlockSpec((1, tt, d), lambda bi, j: (bi, j, 0)),
            pl.BlockSpec((1, tt, LANE), lambda bi, j: (bi, j, 0)),
        ],
        out_shape=[jax.ShapeDtypeStruct((b, s, d), bf16), jax.ShapeDtypeStruct((b, s, LANE), f32)],
        compiler_params=_cparams(("parallel", "arbitrary")),
        name="normmod",
    )(xl, x_ctx, g, mods, w_small)


def _mm_kernel(x_ref, w_ref, o_ref):
    o_ref[...] = jnp.dot(x_ref[...], w_ref[...], preferred_element_type=f32).astype(o_ref.dtype)


def matmul(x, w, out_dtype, tm, tn):
    m, k = x.shape
    n = w.shape[1]
    assert m % tm == 0 and n % tn == 0
    return pl.pallas_call(
        _mm_kernel,
        grid=(m // tm, n // tn),
        in_specs=[pl.BlockSpec((tm, k), lambda i, j: (i, 0)), pl.BlockSpec((k, tn), lambda i, j: (0, j))],
        out_specs=pl.BlockSpec((tm, tn), lambda i, j: (i, j)),
        out_shape=jax.ShapeDtypeStruct((m, n), out_dtype),
        compiler_params=_cparams(("parallel", "arbitrary")),
        name="matmul",
    )(x, w)


def _conv_kernel(u_ref, w_ref, b_ref, o_ref, pad_ref, *, ctx, tile):
    s, cb = u_ref.shape[1], u_ref.shape[2]
    half = CONV_W // 2
    zero8 = jnp.zeros((8, cb), f32)
    pad_ref[0:8, :] = zero8
    pad_ref[8 + ctx:16 + ctx, :] = zero8
    pad_ref[16 + s:24 + s, :] = zero8
    for r0 in range(0, s, tile):
        base = 8 if r0 < ctx else 16
        pad_ref[base + r0:base + r0 + tile, :] = u_ref[0, r0:r0 + tile, :].astype(f32)
    w = w_ref[...]
    bias = b_ref[...]
    for r0 in range(0, s, tile):
        base = 8 if r0 < ctx else 16
        acc = bias + w[0:1, :] * pad_ref[base + r0 - half:base + r0 - half + tile, :]
        for k in range(1, CONV_W):
            acc = acc + w[k:k + 1, :] * pad_ref[base + r0 - half + k:base + r0 - half + k + tile, :]
        o_ref[0, r0:r0 + tile, :] = (acc * jax.nn.sigmoid(acc)).astype(o_ref.dtype)


def conv_silu(u, w, bias, lane_off, cb, ctx):
    b, s, _ = u.shape
    c = w.shape[1]
    assert c % cb == 0 and lane_off % cb == 0
    tile = ROW_TILE
    assert ctx % tile == 0 and s % tile == 0
    off = lane_off // cb
    return pl.pallas_call(
        functools.partial(_conv_kernel, ctx=ctx, tile=tile),
        grid=(b, c // cb),
        in_specs=[
            pl.BlockSpec((1, s, cb), lambda bi, j: (bi, 0, off + j)),
            pl.BlockSpec((CONV_W, cb), lambda bi, j: (0, j)),
            pl.BlockSpec((1, cb), lambda bi, j: (0, j)),
        ],
        out_specs=pl.BlockSpec((1, s, cb), lambda bi, j: (bi, 0, j)),
        out_shape=jax.ShapeDtypeStruct((b, s, c), bf16),
        scratch_shapes=[pltpu.VMEM((s + 24, cb), f32)],
        compiler_params=_cparams(("parallel", "parallel")),
        name="conv_silu",
    )(u, w, bias)


def _split3(a):
    hi = a.astype(bf16)
    r1 = a - hi.astype(f32)
    mid = r1.astype(bf16)
    lo = (r1 - mid.astype(f32)).astype(bf16)
    return jnp.concatenate([hi, mid, lo], axis=0)


def _softplus(x):
    return jnp.maximum(x, 0.0) + jnp.log1p(jnp.exp(-jnp.abs(x)))


def _dot_t(a, b):
    return lax.dot_general(a, b, (((0,), (0,)), ((), ())), preferred_element_type=f32)


def _dot_nt(a, b):
    return lax.dot_general(a, b, (((1,), (1,)), ((), ())), preferred_element_type=f32)


def _ssd_kernel(x_ref, b_ref, c_ref, z_ref, dt_ref, bias_ref, a_ref, dskip_ref, ng_ref, o_ref, yf_ref, h_ref,
                *, nc_ctx, nc, heads, headdim):
    L = SSD_CHUNK
    R, P = heads, headdim
    RP = R * P
    i32 = jnp.int32

    row = lax.broadcasted_iota(i32, (L, L), 0)
    col = lax.broadcasted_iota(i32, (L, L), 1)
    er = lax.broadcasted_iota(i32, (3 * R, R * L), 0) % R
    ec = lax.broadcasted_iota(i32, (3 * R, R * L), 1) // L
    e_blk = (er == ec).astype(bf16)
    fr = lax.broadcasted_iota(i32, (6 * R, 2 * RP), 0)
    fc = lax.broadcasted_iota(i32, (6 * R, 2 * RP), 1)
    e_two = (((fr >= 3 * R) == (fc >= RP)) & ((fr % R) == ((fc % RP) // P))).astype(bf16)
    lane_lo = lax.broadcasted_iota(i32, (L, 2 * P), 1) < P
    ones3 = jnp.ones((3 * R, L), bf16)

    def chunk(ci, d):
        fwd = d == 0
        tri = ((row <= col) if fwd else (row >= col)).astype(bf16)
        mask = (col <= row) if fwd else (col >= row)
        r0 = pl.multiple_of(ci * L, L)
        raw = dt_ref[0, 0, ci, d * R:(d + 1) * R, :]
        dt = _softplus(raw + bias_ref[0, d * R:(d + 1) * R, :])
        a = dt * a_ref[0, d * R:(d + 1) * R, :]
        cum3 = jnp.dot(_split3(a), tri, preferred_element_type=f32)
        cum = cum3[0:R] + cum3[R:2 * R] + cum3[2 * R:3 * R]
        total = cum[:, L - 1:L] if fwd else cum[:, 0:1]
        ecum = jnp.exp(cum)
        ws = jnp.exp(total - cum) * dt
        c3 = _split3(cum)
        lhs1 = jnp.concatenate([c3, ones3], axis=0)
        rhs1 = jnp.concatenate([e_blk, jnp.tile(-c3, (1, R)) * e_blk], axis=0)
        arg = _dot_t(lhs1, rhs1)
        two = _dot_t(jnp.concatenate([_split3(ecum), _split3(ws)], axis=0), e_two)
        ecum_x, ws_x = two[:, :RP], two[:, RP:]
        etot_x = ecum_x[L - 1:L, :] if fwd else ecum_x[0:1, :]

        xc = x_ref[0, pl.ds(r0, L), :]
        bc = b_ref[0, pl.ds(r0, L), :]
        cc = c_ref[0, pl.ds(r0, L), :]
        cb = _dot_nt(cc, bc)
        ys = []
        for p in range(R // 2):
            w_pair = []
            for h in (2 * p, 2 * p + 1):
                dec = jnp.exp(jnp.where(mask, arg[:, h * L:(h + 1) * L], -jnp.inf))
                w_pair.append((dec * (cb * dt[h:h + 1, :])).astype(bf16))
            xp = xc[:, 2 * p * P:2 * (p + 1) * P]
            zero = jnp.zeros_like(xp)
            rhs = jnp.concatenate([jnp.where(lane_lo, xp, zero), jnp.where(lane_lo, zero, xp)], axis=0)
            ys.append(jnp.dot(jnp.concatenate(w_pair, axis=1), rhs, preferred_element_type=f32))
        y_intra = jnp.concatenate(ys, axis=1)
        ht = h_ref[...]
        y = y_intra + jnp.dot(cc, ht.astype(bf16), preferred_element_type=f32) * ecum_x
        xw = (xc.astype(f32) * ws_x).astype(bf16)
        h_ref[...] = ht * etot_x + _dot_t(bc, xw)
        return r0, y, xc

    h_ref[...] = jnp.zeros_like(h_ref)

    def fwd_body(s, carry):
        r0, y, _ = chunk(s, 0)
        yf_ref[pl.ds(r0, L), :] = y
        return carry

    lax.fori_loop(0, nc, fwd_body, 0)

    h_ref[...] = jnp.zeros_like(h_ref)

    def bwd_body(s, carry):
        ci = jnp.where(s < nc_ctx, nc_ctx - 1 - s, nc + nc_ctx - 1 - s)
        r0, y, xc = chunk(ci, 1)
        y = y + yf_ref[pl.ds(r0, L), :] + dskip_ref[...] * xc.astype(f32)
        z = z_ref[0, pl.ds(r0, L), :].astype(f32)
        y = y * (z * jax.nn.sigmoid(z))
        y = y * lax.rsqrt(jnp.mean(y * y, axis=-1, keepdims=True) + NORM_EPS)
        o_ref[0, pl.ds(r0, L), :] = (y * ng_ref[...]).astype(o_ref.dtype)
        return carry

    lax.fori_loop(0, nc, bwd_body, 0)


def ssd_mixer(xbc, z_src, dt_small, dt_bias, a_log, d_skip, norm_g, ctx):
    b, s, _ = xbc.shape
    n = M_D_STATE
    nh = a_log.shape[-1]
    inner = norm_g.shape[-1]
    p = inner // nh
    g = (xbc.shape[-1] - inner) // (2 * n)
    r = nh // g
    rp = r * p
    L = SSD_CHUNK
    nc, nc_ctx = s // L, ctx // L
    assert p * 2 == LANE and s % L == 0 and ctx % L == 0 and r % 2 == 0 and rp % LANE == 0
    dt_t = dt_small[..., :2 * nh].reshape(b, nc, L, 2, g, r).transpose(0, 4, 1, 3, 5, 2).reshape(b, g, nc, 2 * r, L)

    def per_head(v):
        v = v.astype(f32).reshape(2, g, r).transpose(1, 0, 2).reshape(g, 2 * r, 1)
        return jnp.broadcast_to(v, (g, 2 * r, L))

    bias_t = per_head(dt_bias)
    a_t = per_head(-jnp.exp(a_log.astype(f32)))
    dskip = jnp.repeat(d_skip.astype(f32), p)[None, :]
    ng = norm_g.astype(f32)[None, :]
    boff = inner // n
    return pl.pallas_call(
        functools.partial(_ssd_kernel, nc_ctx=nc_ctx, nc=nc, heads=r, headdim=p),
        grid=(b, g),
        in_specs=[
            pl.BlockSpec((1, s, rp), lambda bi, gi: (bi, 0, gi)),
            pl.BlockSpec((1, s, n), lambda bi, gi: (bi, 0, boff + gi)),
            pl.BlockSpec((1, s, n), lambda bi, gi: (bi, 0, boff + g + gi)),
            pl.BlockSpec((1, s, rp), lambda bi, gi: (bi, 0, gi)),
            pl.BlockSpec((1, 1, nc, 2 * r, L), lambda bi, gi: (bi, gi, 0, 0, 0)),
            pl.BlockSpec((1, 2 * r, L), lambda bi, gi: (gi, 0, 0)),
            pl.BlockSpec((1, 2 * r, L), lambda bi, gi: (gi, 0, 0)),
            pl.BlockSpec((1, rp), lambda bi, gi: (0, gi)),
            pl.BlockSpec((1, rp), lambda bi, gi: (0, gi)),
        ],
        out_specs=pl.BlockSpec((1, s, rp), lambda bi, gi: (bi, 0, gi)),
        out_shape=jax.ShapeDtypeStruct((b, s, inner), bf16),
        scratch_shapes=[pltpu.VMEM((s, rp), f32), pltpu.VMEM((n, rp), f32)],
        compiler_params=_cparams(("parallel", "parallel")),
        name="ssd_mixer",
    )(xbc, xbc, xbc, z_src, dt_t, bias_t, a_t, dskip, ng)


def _stack_rows(rows, n):
    w = next(r for r in rows if r is not None).shape[1]
    rid = lax.broadcasted_iota(jnp.int32, (n, w), 0)
    out = jnp.zeros((n, w), f32)
    for i, r in enumerate(rows):
        if r is not None:
            out = jnp.where(rid == i, jnp.broadcast_to(r, (n, w)), out)
    return out


def _parts3(a):
    hi = a.astype(bf16).astype(f32)
    mid = (a - hi).astype(bf16).astype(f32)
    lo = a - hi - mid
    return [hi, mid, lo]


def _mlstm_kernel(q_ref, k_ref, v_ref, o_ref, g_ref, gb_ref, ng_ref, out_ref, hf_ref, c_ref,
                  *, nc_ctx, nc, dk, dv):
    L = MLSTM_CHUNK
    i32 = jnp.int32
    scale = dk ** -0.5
    row = lax.broadcasted_iota(i32, (L, L), 0)
    col = lax.broadcasted_iota(i32, (L, L), 1)
    rr = lax.broadcasted_iota(i32, (16, L + LANE), 0)
    rc = lax.broadcasted_iota(i32, (16, L + LANE), 1)
    rhs_const = (((rr < 3) & (rc <= L)) | ((rr >= 6) & (rr < 9) & (rc == L + 1))).astype(f32)
    lane_e = lax.broadcasted_iota(i32, (L, LANE), 1)
    one = jnp.ones((1, L), f32)

    def chunk(ci, d, m):
        fwd = d == 0
        tri = ((row <= col) if fwd else (row >= col)).astype(bf16)
        mask = (col <= row) if fwd else (col >= row)
        r0 = pl.multiple_of(ci * L, L)
        gts = g_ref[0, 0, ci] + gb_ref[0]
        ip = gts[2 * d:2 * d + 1, :]
        logf = -_softplus(-gts[2 * d + 1:2 * d + 2, :])
        cum3 = jnp.dot(_stack_rows(_parts3(logf), 16).astype(bf16), tri, preferred_element_type=f32)
        bb = cum3[0:1] + cum3[1:2] + cum3[2:3]
        btot = bb[:, L - 1:L] if fwd else bb[:, 0:1]
        u = ip - bb
        lhs = _stack_rows(_parts3(bb) + [one, one, one] + _parts3(u), 16).astype(bf16)
        u_rows = _stack_rows([None, None, None] + _parts3(u), 16)
        rhs = (jnp.concatenate([u_rows, jnp.zeros((16, LANE), f32)], axis=1) + rhs_const).astype(bf16)
        arg = _dot_t(lhs, rhs)
        b_col, u_col = arg[:, L:L + 1], arg[:, L + 1:L + 2]
        dm = jnp.where(mask, arg[:, :L], -jnp.inf)
        inter = b_col + m
        m_out = jnp.maximum(inter, jnp.max(dm, axis=1, keepdims=True))
        q = q_ref[0, pl.ds(r0, L), :]
        k = k_ref[0, pl.ds(r0, L), :]
        v = v_ref[0, pl.ds(r0, L), :]
        w = jnp.exp(dm - m_out) * (_dot_nt(q, k) * scale)
        inter_w = jnp.exp(inter - m_out)
        cx = c_ref[...]
        qc = jnp.dot(q, cx.astype(bf16), preferred_element_type=f32) * scale
        num = jnp.dot(w.astype(bf16), v, preferred_element_type=f32) + inter_w * qc[:, :dv]
        den = jnp.sum(w, axis=1, keepdims=True) + inter_w * (qc[:, dv:dv + 1] + qc[:, dv + 1:dv + 2])
        hk = num / jnp.maximum(jnp.abs(den), jnp.exp(-m_out))
        m_new = jnp.maximum(btot + m, btot + jnp.max(u, axis=1, keepdims=True))
        e_col = jnp.exp(u_col + btot - m_new)
        e_hi = e_col.astype(bf16).astype(f32)
        e_ext = jnp.where(lane_e == 0, e_hi, jnp.where(lane_e == 1, e_col - e_hi, 0.0))
        vext = jnp.concatenate([v.astype(f32) * e_col, e_ext], axis=1).astype(bf16)
        c_ref[...] = jnp.exp(btot + m - m_new) * cx + _dot_t(k, vext)
        return r0, hk, m_new

    m0 = jnp.full((1, 1), -jnp.inf, f32)
    c_ref[...] = jnp.zeros_like(c_ref)

    def fwd_body(s, m):
        r0, hk, m = chunk(s, 0, m)
        hf_ref[pl.ds(r0, L), :] = hk
        return m

    lax.fori_loop(0, nc, fwd_body, m0)

    c_ref[...] = jnp.zeros_like(c_ref)

    def bwd_body(s, m):
        ci = jnp.where(s < nc_ctx, nc_ctx - 1 - s, nc + nc_ctx - 1 - s)
        r0, hk, m = chunk(ci, 1, m)
        y = hk + hf_ref[pl.ds(r0, L), :]
        y = y * lax.rsqrt(jnp.mean(y * y, axis=-1, keepdims=True) + NORM_EPS) * ng_ref[...]
        y = y * jax.nn.sigmoid(o_ref[0, pl.ds(r0, L), :].astype(f32))
        out_ref[0, pl.ds(r0, L), :] = y.astype(out_ref.dtype)
        return m

    lax.fori_loop(0, nc, bwd_body, m0)


def mlstm_mixer(qkv, o_src, o_off, gate_small, gate_b, norm_g, ctx):
    b, s, _ = qkv.shape
    nh = gate_b.shape[-1]
    vtot = norm_g.shape[-1]
    dv = vtot // nh
    dk = (qkv.shape[-1] - vtot) // (2 * nh)
    L = MLSTM_CHUNK
    nc, nc_ctx = s // L, ctx // L
    assert s % L == 0 and ctx % L == 0 and dk % LANE == 0 and dv % LANE == 0 and o_off % dv == 0
    gt = gate_small[..., :4 * nh].reshape(b, nc, L, 4, nh).transpose(0, 4, 1, 3, 2)
    gt = jnp.pad(gt, ((0, 0), (0, 0), (0, 0), (0, 4), (0, 0)))
    gb = jnp.pad(gate_b.astype(f32).reshape(4, nh).T, ((0, 0), (0, 4)))
    gb = jnp.broadcast_to(gb[:, :, None], (nh, 8, L))
    ng = norm_g.astype(f32)[None, :]
    koff = nh
    voff = 2 * nh * dk // dv
    ooff = o_off // dv
    return pl.pallas_call(
        functools.partial(_mlstm_kernel, nc_ctx=nc_ctx, nc=nc, dk=dk, dv=dv),
        grid=(b, nh),
        in_specs=[
            pl.BlockSpec((1, s, dk), lambda bi, hi: (bi, 0, hi)),
            pl.BlockSpec((1, s, dk), lambda bi, hi: (bi, 0, koff + hi)),
            pl.BlockSpec((1, s, dv), lambda bi, hi: (bi, 0, voff + hi)),
            pl.BlockSpec((1, s, dv), lambda bi, hi: (bi, 0, ooff + hi)),
            pl.BlockSpec((1, 1, nc, 8, L), lambda bi, hi: (bi, hi, 0, 0, 0)),
            pl.BlockSpec((1, 8, L), lambda bi, hi: (hi, 0, 0)),
            pl.BlockSpec((1, dv), lambda bi, hi: (0, hi)),
        ],
        out_specs=pl.BlockSpec((1, s, dv), lambda bi, hi: (bi, 0, hi)),
        out_shape=jax.ShapeDtypeStruct((b, s, vtot), bf16),
        scratch_shapes=[pltpu.VMEM((s, dv), f32), pltpu.VMEM((dk, dv + LANE), f32)],
        compiler_params=_cparams(("parallel", "parallel")),
        name="mlstm_mixer",
    )(qkv, qkv, qkv, o_src, gt, gb, ng)


def _outproj_kernel(y_ref, w_ref, xl_ref, xc_ref, g_ref, ol_ref, oc_ref, *, n_ctx, cols, d):
    j = pl.program_id(1)
    acc = jnp.dot(y_ref[0], w_ref[...], preferred_element_type=f32)

    @pl.when(j < n_ctx)
    def _():
        oc_ref[0] = xc_ref[0] + g_ref[0, 0:1, :] * acc

    @pl.when(j >= n_ctx)
    def _():
        gl = g_ref[0, 1:2, :]
        if cols == 0:
            ol_ref[0] = xl_ref[0] + gl * acc
        else:
            nr = xl_ref.shape[1]
            for ci in range(cols):
                ol_ref[0, :, ci * d:(ci + 1) * d] = xl_ref[0, :, ci * d:(ci + 1) * d] + gl * acc[ci * nr:(ci + 1) * nr, :]


def outproj_residual(y, w, x_lat, x_ctx, gates, col_major):
    b, t, d = x_lat.shape
    ctx = x_ctx.shape[1]
    kin = y.shape[-1]
    tt = ROW_TILE
    n_ctx, n_lat = ctx // tt, t // tt
    if col_major:
        r = t // GRID_W
        cols = tt // r
        xl = x_lat.reshape(b, r, GRID_W * d)
        xl_spec = pl.BlockSpec((1, r, cols * d), lambda bi, j: (bi, 0, jnp.maximum(j - n_ctx, 0)))
    else:
        cols = 0
        xl = x_lat
        xl_spec = pl.BlockSpec((1, tt, d), lambda bi, j: (bi, jnp.maximum(j - n_ctx, 0), 0))
    ol, oc = pl.pallas_call(
        functools.partial(_outproj_kernel, n_ctx=n_ctx, cols=cols, d=d),
        grid=(b, n_ctx + n_lat),
        in_specs=[
            pl.BlockSpec((1, tt, kin), lambda bi, j: (bi, j, 0)),
            pl.BlockSpec((kin, d), lambda bi, j: (0, 0), pipeline_mode=pl.Buffered(1)),
            xl_spec,
            pl.BlockSpec((1, tt, d), lambda bi, j: (bi, jnp.minimum(j, n_ctx - 1), 0)),
            pl.BlockSpec((1, 2, d), lambda bi, j: (bi, 0, 0)),
        ],
        out_specs=[xl_spec, pl.BlockSpec((1, tt, d), lambda bi, j: (bi, jnp.minimum(j, n_ctx - 1), 0))],
        out_shape=[jax.ShapeDtypeStruct(xl.shape, f32), jax.ShapeDtypeStruct(x_ctx.shape, f32)],
        compiler_params=_cparams(("parallel", "arbitrary")),
        name="outproj_residual",
    )(y, w, xl, x_ctx, gates)
    return ol.reshape(b, t, d), oc


def _mod_kernel(c_ref, w_ref, b_ref, o_ref):
    cvec = c_ref[...]
    cs = (cvec * jax.nn.sigmoid(cvec)).astype(bf16)
    o_ref[0] = jnp.dot(cs, w_ref[0].astype(bf16), preferred_element_type=f32) + b_ref[0]


def modulation(c_rows, mod_w, mod_b):
    depth, d, n = mod_w.shape
    tn = 1024
    assert n % tn == 0
    return pl.pallas_call(
        _mod_kernel,
        grid=(depth, n // tn),
        in_specs=[
            pl.BlockSpec(c_rows.shape, lambda i, j: (0, 0)),
            pl.BlockSpec((1, d, tn), lambda i, j: (i, 0, j)),
            pl.BlockSpec((1, 1, tn), lambda i, j: (i, 0, j)),
        ],
        out_specs=pl.BlockSpec((1, c_rows.shape[0], tn), lambda i, j: (i, 0, j)),
        out_shape=jax.ShapeDtypeStruct((depth, c_rows.shape[0], n), f32),
        compiler_params=_cparams(("parallel", "parallel")),
        name="modulation",
    )(c_rows, mod_w, mod_b.reshape(depth, 1, n))


def _moe_kernel(be_ref, nu_ref, x_ref, wgu_ref, bgu_ref, wd_ref, bd_ref, o_ref, *, f):
    i = pl.program_id(0)

    @pl.when(i < nu_ref[0])
    def _():
        h = jnp.dot(x_ref[...], wgu_ref[0], preferred_element_type=f32) + bgu_ref[0]
        glu = jnp.minimum(h[:, :f], SWIGLU_LIMIT)
        lin = jnp.clip(h[:, f:], -SWIGLU_LIMIT, SWIGLU_LIMIT)
        act = glu * jax.nn.sigmoid(SWIGLU_ALPHA * glu) * (lin + 1.0)
        y = jnp.dot(act.astype(bf16), wd_ref[0], preferred_element_type=f32) + bd_ref[0]
        o_ref[...] = y.astype(o_ref.dtype)

    @pl.when(i >= nu_ref[0])
    def _():
        o_ref[...] = jnp.zeros_like(o_ref)


def moe_experts(xg, block_e, n_used, gu_w, gu_b, down_w, down_b, tm):
    n_slots, d = xg.shape
    e, _, f2 = gu_w.shape
    f = f2 // 2
    n_blocks = n_slots // tm
    last = lambda i, nu: jnp.minimum(i, nu[0] - 1)
    grid_spec = pltpu.PrefetchScalarGridSpec(
        num_scalar_prefetch=2,
        grid=(n_blocks,),
        in_specs=[
            pl.BlockSpec((tm, d), lambda i, be, nu: (last(i, nu), 0)),
            pl.BlockSpec((1, d, f2), lambda i, be, nu: (be[last(i, nu)], 0, 0)),
            pl.BlockSpec((1, 1, f2), lambda i, be, nu: (be[last(i, nu)], 0, 0)),
            pl.BlockSpec((1, f, d), lambda i, be, nu: (be[last(i, nu)], 0, 0)),
            pl.BlockSpec((1, 1, d), lambda i, be, nu: (be[last(i, nu)], 0, 0)),
        ],
        out_specs=pl.BlockSpec((tm, d), lambda i, be, nu: (i, 0)),
    )
    return pl.pallas_call(
        functools.partial(_moe_kernel, f=f),
        grid_spec=grid_spec,
        out_shape=jax.ShapeDtypeStruct((n_slots, d), bf16),
        compiler_params=_cparams(("arbitrary",)),
        name="moe_experts",
    )(block_e, n_used, xg, gu_w, gu_b.reshape(e, 1, f2), down_w, down_b.reshape(e, 1, d))


def _final_norm_kernel(x_ref, g_ref, o_ref):
    x = x_ref[0]
    o_ref[0] = (x * lax.rsqrt(jnp.mean(x * x, axis=-1, keepdims=True) + NORM_EPS)) * g_ref[...]


def final_norm(x, g):
    b, t, d = x.shape
    tt = 512 if t % 512 == 0 else ROW_TILE
    return pl.pallas_call(
        _final_norm_kernel,
        grid=(b, t // tt),
        in_specs=[pl.BlockSpec((1, tt, d), lambda bi, j: (bi, j, 0)), pl.BlockSpec((1, d), lambda bi, j: (0, 0))],
        out_specs=pl.BlockSpec((1, tt, d), lambda bi, j: (bi, j, 0)),
        out_shape=jax.ShapeDtypeStruct((b, t, d), f32),
        compiler_params=_cparams(("parallel", "parallel")),
        name="final_norm",
    )(x, g)


MOE_TILE = 512


def _pick_tile(n, prefs):
    for p in prefs:
        if n % p == 0:
            return p
    raise ValueError(f"no tile in {prefs} divides {n}")


def _pad_lanes(w):
    return jnp.pad(w, ((0, 0), (0, LANE - w.shape[1])))


def _project(h, w_main):
    b, s, d = h.shape
    m = b * s
    n = w_main.shape[1]
    out = matmul(h.reshape(m, d), w_main, bf16, _pick_tile(m, (1024, 512, 256)), _pick_tile(n, (1024, 512, 256, 128)))
    return out.reshape(b, s, n)


def _moe_ffn(tokens, logits, gu_w, gu_b, down_w, down_b):
    n, d = tokens.shape
    e = gu_w.shape[0]
    tm = MOE_TILE
    i32 = jnp.int32
    top_logit, top_idx = lax.top_k(logits, TOP_K)
    gates = jax.nn.softmax(top_logit, axis=-1)
    n_assign = n * TOP_K
    flat_e = top_idx.reshape(-1).astype(i32)
    order = jnp.argsort(flat_e).astype(i32)
    sorted_e = flat_e[order]
    counts = jnp.bincount(flat_e, length=e).astype(i32)
    padded = (counts + tm - 1) // tm * tm
    start = jnp.cumsum(counts) - counts
    pad_end = jnp.cumsum(padded)
    pad_start = pad_end - padded
    slot = (jnp.arange(n_assign, dtype=i32) - start[sorted_e] + pad_start[sorted_e]).astype(i32)
    n_blocks = -(-n_assign // tm) + e
    n_slots = n_blocks * tm
    slot_tok = jnp.zeros((n_slots,), i32).at[slot].set(order // TOP_K)
    inv_slot = jnp.zeros((n_assign,), i32).at[order].set(slot)
    block_e = jnp.minimum(jnp.searchsorted(pad_end, jnp.arange(n_blocks, dtype=i32) * tm, side='right'), e - 1).astype(i32)
    n_used = (pad_end[-1] // tm).astype(i32).reshape(1)
    xg = jnp.take(tokens, slot_tok, axis=0)
    ys = moe_experts(xg, block_e, n_used, gu_w, gu_b, down_w, down_b, tm)
    yk = jnp.take(ys, inv_slot, axis=0).reshape(n, TOP_K, d).astype(f32)
    return jnp.sum(yk * gates[..., None], axis=1)


def kernel(x, c, ctx, c_ctx, mod_w, mod_b, norm1_g, norm2_g, final_g, m_in_w, m_conv_w, m_conv_b, m_dt_bias, m_a_log, m_d, m_norm_g, m_out_w, l_in_w, l_conv_w, l_conv_b, l_gate_b, l_norm_g, l_out_w, router_w, router_b, e_gu_w, e_gu_b, e_down_w, e_down_b):
    b, t, d = x.shape
    n_ctx_tok = ctx.shape[1]
    depth = mod_w.shape[0]
    n_exp = router_w.shape[-1]
    m_inner = m_out_w.shape[1]
    m_conv = m_conv_w.shape[-1]
    l_qkv = l_conv_w.shape[-1]
    l_vtot = l_out_w.shape[1]

    c_rows = jnp.zeros((16, d), f32).at[:b].set(c).at[b].set(c_ctx)
    mod_all = modulation(c_rows, mod_w, mod_b)

    xl, xc = x, ctx
    for i in range(depth):
        j = i // 2
        mod = mod_all[i].reshape(16, 6, d)
        lat, cx = mod[:b], jnp.broadcast_to(mod[b][None], (b, 6, d))

        def mods(k):
            return jnp.stack([cx[:, k], cx[:, k + 1], lat[:, k], lat[:, k + 1]], axis=1)

        def gate(k):
            return jnp.stack([cx[:, k], lat[:, k]], axis=1)

        if i % 2 == 0:
            w_in = m_in_w[j].astype(bf16)
            n_main = m_inner + m_conv
            h, dt_small = normmod(xl, xc, norm1_g[i][None], mods(0), _pad_lanes(w_in[:, n_main:]), False)
            proj = _project(h, w_in[:, :n_main])
            xbc = conv_silu(proj, m_conv_w[j], m_conv_b[j][None], m_inner, 512, n_ctx_tok)
            y = ssd_mixer(xbc, proj, dt_small, m_dt_bias[j], m_a_log[j], m_d[j], m_norm_g[j], n_ctx_tok)
            xl, xc = outproj_residual(y, m_out_w[j].astype(bf16), xl, xc, gate(2), False)
        else:
            w_in = l_in_w[j].astype(bf16)
            n_main = l_qkv + l_vtot
            h, gate_small = normmod(xl, xc, norm1_g[i][None], mods(0), _pad_lanes(w_in[:, n_main:]), True)
            proj = _project(h, w_in[:, :n_main])
            qkv = conv_silu(proj, l_conv_w[j], l_conv_b[j][None], 0, 512, n_ctx_tok)
            y = mlstm_mixer(qkv, proj, l_qkv, gate_small, l_gate_b[j], l_norm_g[j], n_ctx_tok)
            xl, xc = outproj_residual(y, l_out_w[j].astype(bf16), xl, xc, gate(2), True)

        u, logits = normmod(xl, xc, norm2_g[i][None], mods(3), _pad_lanes(router_w[i].astype(bf16)), False)
        s = n_ctx_tok + t
        logits = logits.reshape(b * s, LANE)[:, :n_exp] + router_b[i].astype(f32)
        f = _moe_ffn(u.reshape(b * s, d), logits, e_gu_w[i].astype(bf16), e_gu_b[i], e_down_w[i].astype(bf16), e_down_b[i])
        f = f.reshape(b, s, d)
        g2 = gate(5)
        xl = xl + g2[:, 1:2] * f[:, n_ctx_tok:]
        xc = xc + g2[:, 0:1] * f[:, :n_ctx_tok]
    return final_norm(xl, final_g[None])
```

```python
import functools

import jax
import jax.numpy as jnp
from jax import lax
from jax.experimental import pallas as pl
from jax.experimental.pallas import tpu as pltpu

f32 = jnp.float32
bf16 = jnp.bfloat16

NORM_EPS = 1e-6
CONV_W = 5
GRID_W = 64
M_D_STATE = 128
TOP_K = 4
SWIGLU_LIMIT = 7.0
SWIGLU_ALPHA = 1.702
SSD_CHUNK = 128
MLSTM_CHUNK = 256
ROW_TILE = 256
LANE = 128
V7X_VMEM_LIMIT = 60000 * 1024


def _cparams(sem):
    return pltpu.CompilerParams(dimension_semantics=sem, vmem_limit_bytes=V7X_VMEM_LIMIT)


def _pack_pair(a, b):
    u32 = jnp.uint32
    return (lax.bitcast_convert_type(a, u32) & u32(0xFFFF0000)) | (lax.bitcast_convert_type(b, u32) >> 16)


def _unpack_pair(w):
    u32 = jnp.uint32
    return (lax.bitcast_convert_type(w & u32(0xFFFF0000), f32), lax.bitcast_convert_type(w << 16, f32))


def _normmod_kernel(xl_ref, xc_ref, g_ref, mod_ref, ws_ref, h_ref, small_ref, *, n_ctx, cols, d, pack):
    j = pl.program_id(1)

    def rows(x, shift, scale, r0, nr):
        ms = jnp.mean(x * x, axis=-1, keepdims=True)
        y = (x * lax.rsqrt(ms + NORM_EPS)) * g_ref[...]
        hb = (y * (1.0 + scale) + shift).astype(bf16)
        if pack:
            hf = hb.astype(f32)
            h_ref[0, r0:r0 + nr, :] = _pack_pair(hf[:, :d // 2], hf[:, d // 2:])
        else:
            h_ref[0, r0:r0 + nr, :] = hb
        small_ref[0, r0:r0 + nr, :] = jnp.dot(hb, ws_ref[...], preferred_element_type=f32)

    @pl.when(j < n_ctx)
    def _():
        rows(xc_ref[0], mod_ref[0, 0:1, :], mod_ref[0, 1:2, :], 0, xc_ref.shape[1])

    @pl.when(j >= n_ctx)
    def _():
        if cols == 0:
            rows(xl_ref[0], mod_ref[0, 2:3, :], mod_ref[0, 3:4, :], 0, xl_ref.shape[1])
        else:
            nr = xl_ref.shape[1]
            for ci in range(cols):
                rows(xl_ref[0, :, ci * d:(ci + 1) * d], mod_ref[0, 2:3, :], mod_ref[0, 3:4, :], ci * nr, nr)


def normmod(x_lat, x_ctx, g, mods, w_small, col_major, pack=False):
    b, t, d = x_lat.shape
    ctx = x_ctx.shape[1]
    tt = ROW_TILE
    n_ctx, n_lat = ctx // tt, t // tt
    assert ctx % tt == 0 and t % tt == 0
    if col_major:
        r = t // GRID_W
        cols = tt // r
        assert tt % r == 0 and GRID_W % cols == 0
        xl = x_lat.reshape(b, r, GRID_W * d)
        xl_spec = pl.BlockSpec((1, r, cols * d), lambda bi, j: (bi, 0, jnp.maximum(j - n_ctx, 0)))
    else:
        cols = 0
        xl = x_lat
        xl_spec = pl.BlockSpec((1, tt, d), lambda bi, j: (bi, jnp.maximum(j - n_ctx, 0), 0))
    s = ctx + t
    dh = d // 2 if pack else d
    return pl.pallas_call(
        functools.partial(_normmod_kernel, n_ctx=n_ctx, cols=cols, d=d, pack=pack),
        grid=(b, n_ctx + n_lat),
        in_specs=[
            xl_spec,
            pl.BlockSpec((1, tt, d), lambda bi, j: (bi, jnp.minimum(j, n_ctx - 1), 0)),
            pl.BlockSpec((1, d), lambda bi, j: (0, 0)),
            pl.BlockSpec((1, 4, d), lambda bi, j: (bi, 0, 0)),
            pl.BlockSpec((d, LANE), lambda bi, j: (0, 0)),
        ],
        out_specs=[
            pl.BlockSpec((1, tt, dh), lambda bi, j: (bi, j, 0)),
            pl.BlockSpec((1, tt, LANE), lambda bi, j: (bi, j, 0)),
        ],
        out_shape=[jax.ShapeDtypeStruct((b, s, dh), jnp.uint32 if pack else bf16),
                   jax.ShapeDtypeStruct((b, s, LANE), f32)],
        compiler_params=_cparams(("parallel", "arbitrary")),
        name="normmod",
    )(xl, x_ctx, g, mods, w_small)


def _mm_kernel(x_ref, w_ref, o_ref):
    o_ref[...] = jnp.dot(x_ref[...], w_ref[...], preferred_element_type=f32).astype(o_ref.dtype)


def matmul(x, w, out_dtype, tm, tn):
    m, k = x.shape
    n = w.shape[1]
    assert m % tm == 0 and n % tn == 0
    return pl.pallas_call(
        _mm_kernel,
        grid=(m // tm, n // tn),
        in_specs=[pl.BlockSpec((tm, k), lambda i, j: (i, 0)), pl.BlockSpec((k, tn), lambda i, j: (0, j))],
        out_specs=pl.BlockSpec((tm, tn), lambda i, j: (i, j)),
        out_shape=jax.ShapeDtypeStruct((m, n), out_dtype),
        compiler_params=_cparams(("parallel", "arbitrary")),
        name="matmul",
    )(x, w)


def _conv_kernel(u_ref, w_ref, b_ref, o_ref, pad_ref, *, ctx, tile):
    s, cb = u_ref.shape[1], u_ref.shape[2]
    half = CONV_W // 2
    zero8 = jnp.zeros((8, cb), f32)
    pad_ref[0:8, :] = zero8
    pad_ref[8 + ctx:16 + ctx, :] = zero8
    pad_ref[16 + s:24 + s, :] = zero8
    for r0 in range(0, s, tile):
        base = 8 if r0 < ctx else 16
        pad_ref[base + r0:base + r0 + tile, :] = u_ref[0, r0:r0 + tile, :].astype(f32)
    w = w_ref[...]
    bias = b_ref[...]
    for r0 in range(0, s, tile):
        base = 8 if r0 < ctx else 16
        acc = bias + w[0:1, :] * pad_ref[base + r0 - half:base + r0 - half + tile, :]
        for k in range(1, CONV_W):
            acc = acc + w[k:k + 1, :] * pad_ref[base + r0 - half + k:base + r0 - half + k + tile, :]
        o_ref[0, r0:r0 + tile, :] = (acc * jax.nn.sigmoid(acc)).astype(o_ref.dtype)


def conv_silu(u, w, bias, lane_off, cb, ctx):
    b, s, _ = u.shape
    c = w.shape[1]
    assert c % cb == 0 and lane_off % cb == 0
    tile = ROW_TILE
    assert ctx % tile == 0 and s % tile == 0
    off = lane_off // cb
    return pl.pallas_call(
        functools.partial(_conv_kernel, ctx=ctx, tile=tile),
        grid=(b, c // cb),
        in_specs=[
            pl.BlockSpec((1, s, cb), lambda bi, j: (bi, 0, off + j)),
            pl.BlockSpec((CONV_W, cb), lambda bi, j: (0, j)),
            pl.BlockSpec((1, cb), lambda bi, j: (0, j)),
        ],
        out_specs=pl.BlockSpec((1, s, cb), lambda bi, j: (bi, 0, j)),
        out_shape=jax.ShapeDtypeStruct((b, s, c), bf16),
        scratch_shapes=[pltpu.VMEM((s + 24, cb), f32)],
        compiler_params=_cparams(("parallel", "parallel")),
        name="conv_silu",
    )(u, w, bias)


def _split3(a):
    hi = a.astype(bf16)
    r1 = a - hi.astype(f32)
    mid = r1.astype(bf16)
    lo = (r1 - mid.astype(f32)).astype(bf16)
    return jnp.concatenate([hi, mid, lo], axis=0)


def _softplus(x):
    return jnp.maximum(x, 0.0) + jnp.log1p(jnp.exp(-jnp.abs(x)))


def _dot_t(a, b):
    return lax.dot_general(a, b, (((0,), (0,)), ((), ())), preferred_element_type=f32)


def _dot_nt(a, b):
    return lax.dot_general(a, b, (((1,), (1,)), ((), ())), preferred_element_type=f32)


def _ssd_kernel(x_ref, b_ref, c_ref, z_ref, dt_ref, bias_ref, a_ref, dskip_ref, ng_ref, o_ref, yf_ref, h_ref,
                *, nc_ctx, nc, heads, headdim):
    L = SSD_CHUNK
    R, P = heads, headdim
    RP = R * P
    i32 = jnp.int32

    row = lax.broadcasted_iota(i32, (L, L), 0)
    col = lax.broadcasted_iota(i32, (L, L), 1)
    er = lax.broadcasted_iota(i32, (3 * R, R * L), 0) % R
    ec = lax.broadcasted_iota(i32, (3 * R, R * L), 1) // L
    e_blk = (er == ec).astype(bf16)
    fr = lax.broadcasted_iota(i32, (6 * R, 2 * RP), 0)
    fc = lax.broadcasted_iota(i32, (6 * R, 2 * RP), 1)
    e_two = (((fr >= 3 * R) == (fc >= RP)) & ((fr % R) == ((fc % RP) // P))).astype(bf16)
    lane_lo = lax.broadcasted_iota(i32, (L, 2 * P), 1) < P
    ones3 = jnp.ones((3 * R, L), bf16)

    def chunk(ci, d):
        fwd = d == 0
        tri = ((row <= col) if fwd else (row >= col)).astype(bf16)
        mask = (col <= row) if fwd else (col >= row)
        r0 = pl.multiple_of(ci * L, L)
        raw = dt_ref[0, 0, ci, d * R:(d + 1) * R, :]
        dt = _softplus(raw + bias_ref[0, d * R:(d + 1) * R, :])
        a = dt * a_ref[0, d * R:(d + 1) * R, :]
        cum3 = jnp.dot(_split3(a), tri, preferred_element_type=f32)
        cum = cum3[0:R] + cum3[R:2 * R] + cum3[2 * R:3 * R]
        total = cum[:, L - 1:L] if fwd else cum[:, 0:1]
        ecum = jnp.exp(cum)
        ws = jnp.exp(total - cum) * dt
        c3 = _split3(cum)
        lhs1 = jnp.concatenate([c3, ones3], axis=0)
        rhs1 = jnp.concatenate([e_blk, jnp.tile(-c3, (1, R)) * e_blk], axis=0)
        arg = _dot_t(lhs1, rhs1)
        two = _dot_t(jnp.concatenate([_split3(ecum), _split3(ws)], axis=0), e_two)
        ecum_x, ws_x = two[:, :RP], two[:, RP:]
        etot_x = ecum_x[L - 1:L, :] if fwd else ecum_x[0:1, :]

        xc = x_ref[0, pl.ds(r0, L), :]
        bc = b_ref[0, pl.ds(r0, L), :]
        cc = c_ref[0, pl.ds(r0, L), :]
        cb = _dot_nt(cc, bc)
        ys = []
        for p in range(R // 2):
            w_pair = []
            for h in (2 * p, 2 * p + 1):
                dec = jnp.exp(jnp.where(mask, arg[:, h * L:(h + 1) * L], -jnp.inf))
                w_pair.append((dec * (cb * dt[h:h + 1, :])).astype(bf16))
            xp = xc[:, 2 * p * P:2 * (p + 1) * P]
            zero = jnp.zeros_like(xp)
            rhs = jnp.concatenate([jnp.where(lane_lo, xp, zero), jnp.where(lane_lo, zero, xp)], axis=0)
            ys.append(jnp.dot(jnp.concatenate(w_pair, axis=1), rhs, preferred_element_type=f32))
        y_intra = jnp.concatenate(ys, axis=1)
        ht = h_ref[...]
        y = y_intra + jnp.dot(cc, ht.astype(bf16), preferred_element_type=f32) * ecum_x
        xw = (xc.astype(f32) * ws_x).astype(bf16)
        h_ref[...] = ht * etot_x + _dot_t(bc, xw)
        return r0, y, xc

    h_ref[...] = jnp.zeros_like(h_ref)

    def fwd_body(s, carry):
        r0, y, _ = chunk(s, 0)
        yf_ref[pl.ds(r0, L), :] = y
        return carry

    lax.fori_loop(0, nc, fwd_body, 0)

    h_ref[...] = jnp.zeros_like(h_ref)

    def bwd_body(s, carry):
        ci = jnp.where(s < nc_ctx, nc_ctx - 1 - s, nc + nc_ctx - 1 - s)
        r0, y, xc = chunk(ci, 1)
        y = y + yf_ref[pl.ds(r0, L), :] + dskip_ref[...] * xc.astype(f32)
        z = z_ref[0, pl.ds(r0, L), :].astype(f32)
        y = y * (z * jax.nn.sigmoid(z))
        y = y * lax.rsqrt(jnp.mean(y * y, axis=-1, keepdims=True) + NORM_EPS)
        o_ref[0, pl.ds(r0, L), :] = (y * ng_ref[...]).astype(o_ref.dtype)
        return carry

    lax.fori_loop(0, nc, bwd_body, 0)


def ssd_mixer(xbc, z_src, dt_small, dt_bias, a_log, d_skip, norm_g, ctx):
    b, s, _ = xbc.shape
    n = M_D_STATE
    nh = a_log.shape[-1]
    inner = norm_g.shape[-1]
    p = inner // nh
    g = (xbc.shape[-1] - inner) // (2 * n)
    r = nh // g
    rp = r * p
    L = SSD_CHUNK
    nc, nc_ctx = s // L, ctx // L
    assert p * 2 == LANE and s % L == 0 and ctx % L == 0 and r % 2 == 0 and rp % LANE == 0
    dt_t = dt_small[..., :2 * nh].reshape(b, nc, L, 2, g, r).transpose(0, 4, 1, 3, 5, 2).reshape(b, g, nc, 2 * r, L)

    def per_head(v):
        v = v.astype(f32).reshape(2, g, r).transpose(1, 0, 2).reshape(g, 2 * r, 1)
        return jnp.broadcast_to(v, (g, 2 * r, L))

    bias_t = per_head(dt_bias)
    a_t = per_head(-jnp.exp(a_log.astype(f32)))
    dskip = jnp.repeat(d_skip.astype(f32), p)[None, :]
    ng = norm_g.astype(f32)[None, :]
    boff = inner // n
    return pl.pallas_call(
        functools.partial(_ssd_kernel, nc_ctx=nc_ctx, nc=nc, heads=r, headdim=p),
        grid=(b, g),
        in_specs=[
            pl.BlockSpec((1, s, rp), lambda bi, gi: (bi, 0, gi)),
            pl.BlockSpec((1, s, n), lambda bi, gi: (bi, 0, boff + gi)),
            pl.BlockSpec((1, s, n), lambda bi, gi: (bi, 0, boff + g + gi)),
            pl.BlockSpec((1, s, rp), lambda bi, gi: (bi, 0, gi)),
            pl.BlockSpec((1, 1, nc, 2 * r, L), lambda bi, gi: (bi, gi, 0, 0, 0)),
            pl.BlockSpec((1, 2 * r, L), lambda bi, gi: (gi, 0, 0)),
            pl.BlockSpec((1, 2 * r, L), lambda bi, gi: (gi, 0, 0)),
            pl.BlockSpec((1, rp), lambda bi, gi: (0, gi)),
            pl.BlockSpec((1, rp), lambda bi, gi: (0, gi)),
        ],
        out_specs=pl.BlockSpec((1, s, rp), lambda bi, gi: (bi, 0, gi)),
        out_shape=jax.ShapeDtypeStruct((b, s, inner), bf16),
        scratch_shapes=[pltpu.VMEM((s, rp), f32), pltpu.VMEM((n, rp), f32)],
        compiler_params=_cparams(("parallel", "parallel")),
        name="ssd_mixer",
    )(xbc, xbc, xbc, z_src, dt_t, bias_t, a_t, dskip, ng)


def _stack_rows(rows, n):
    w = next(r for r in rows if r is not None).shape[1]
    rid = lax.broadcasted_iota(jnp.int32, (n, w), 0)
    out = jnp.zeros((n, w), f32)
    for i, r in enumerate(rows):
        if r is not None:
            out = jnp.where(rid == i, jnp.broadcast_to(r, (n, w)), out)
    return out


def _parts3(a):
    hi = a.astype(bf16).astype(f32)
    mid = (a - hi).astype(bf16).astype(f32)
    lo = a - hi - mid
    return [hi, mid, lo]


def _mlstm_kernel(q_ref, k_ref, v_ref, o_ref, g_ref, gb_ref, ng_ref, out_ref, hf_ref, c_ref,
                  *, nc_ctx, nc, dk, dv):
    L = MLSTM_CHUNK
    i32 = jnp.int32
    scale = dk ** -0.5
    row = lax.broadcasted_iota(i32, (L, L), 0)
    col = lax.broadcasted_iota(i32, (L, L), 1)
    rr = lax.broadcasted_iota(i32, (16, L + LANE), 0)
    rc = lax.broadcasted_iota(i32, (16, L + LANE), 1)
    rhs_const = (((rr < 3) & (rc <= L)) | ((rr >= 6) & (rr < 9) & (rc == L + 1))).astype(f32)
    lane_e = lax.broadcasted_iota(i32, (L, LANE), 1)
    one = jnp.ones((1, L), f32)

    def chunk(ci, d, m):
        fwd = d == 0
        tri = ((row <= col) if fwd else (row >= col)).astype(bf16)
        mask = (col <= row) if fwd else (col >= row)
        r0 = pl.multiple_of(ci * L, L)
        gts = g_ref[0, 0, ci] + gb_ref[0]
        ip = gts[2 * d:2 * d + 1, :]
        logf = -_softplus(-gts[2 * d + 1:2 * d + 2, :])
        cum3 = jnp.dot(_stack_rows(_parts3(logf), 16).astype(bf16), tri, preferred_element_type=f32)
        bb = cum3[0:1] + cum3[1:2] + cum3[2:3]
        btot = bb[:, L - 1:L] if fwd else bb[:, 0:1]
        u = ip - bb
        lhs = _stack_rows(_parts3(bb) + [one, one, one] + _parts3(u), 16).astype(bf16)
        u_rows = _stack_rows([None, None, None] + _parts3(u), 16)
        rhs = (jnp.concatenate([u_rows, jnp.zeros((16, LANE), f32)], axis=1) + rhs_const).astype(bf16)
        arg = _dot_t(lhs, rhs)
        b_col, u_col = arg[:, L:L + 1], arg[:, L + 1:L + 2]
        dm = jnp.where(mask, arg[:, :L], -jnp.inf)
        inter = b_col + m
        m_out = jnp.maximum(inter, jnp.max(dm, axis=1, keepdims=True))
        q = q_ref[0, pl.ds(r0, L), :]
        k = k_ref[0, pl.ds(r0, L), :]
        v = v_ref[0, pl.ds(r0, L), :]
        w = jnp.exp(dm - m_out) * (_dot_nt(q, k) * scale)
        inter_w = jnp.exp(inter - m_out)
        cx = c_ref[...]
        qc = jnp.dot(q, cx.astype(bf16), preferred_element_type=f32) * scale
        num = jnp.dot(w.astype(bf16), v, preferred_element_type=f32) + inter_w * qc[:, :dv]
        den = jnp.sum(w, axis=1, keepdims=True) + inter_w * (qc[:, dv:dv + 1] + qc[:, dv + 1:dv + 2])
        hk = num / jnp.maximum(jnp.abs(den), jnp.exp(-m_out))
        m_new = jnp.maximum(btot + m, btot + jnp.max(u, axis=1, keepdims=True))
        e_col = jnp.exp(u_col + btot - m_new)
        e_hi = e_col.astype(bf16).astype(f32)
        e_ext = jnp.where(lane_e == 0, e_hi, jnp.where(lane_e == 1, e_col - e_hi, 0.0))
        vext = jnp.concatenate([v.astype(f32) * e_col, e_ext], axis=1).astype(bf16)
        c_ref[...] = jnp.exp(btot + m - m_new) * cx + _dot_t(k, vext)
        return r0, hk, m_new

    m0 = jnp.full((1, 1), -jnp.inf, f32)
    c_ref[...] = jnp.zeros_like(c_ref)

    def fwd_body(s, m):
        r0, hk, m = chunk(s, 0, m)
        hf_ref[pl.ds(r0, L), :] = hk
        return m

    lax.fori_loop(0, nc, fwd_body, m0)

    c_ref[...] = jnp.zeros_like(c_ref)

    def bwd_body(s, m):
        ci = jnp.where(s < nc_ctx, nc_ctx - 1 - s, nc + nc_ctx - 1 - s)
        r0, hk, m = chunk(ci, 1, m)
        y = hk + hf_ref[pl.ds(r0, L), :]
        y = y * lax.rsqrt(jnp.mean(y * y, axis=-1, keepdims=True) + NORM_EPS) * ng_ref[...]
        y = y * jax.nn.sigmoid(o_ref[0, pl.ds(r0, L), :].astype(f32))
        out_ref[0, pl.ds(r0, L), :] = y.astype(out_ref.dtype)
        return m

    lax.fori_loop(0, nc, bwd_body, m0)


def mlstm_mixer(qkv, o_src, o_off, gate_small, gate_b, norm_g, ctx):
    b, s, _ = qkv.shape
    nh = gate_b.shape[-1]
    vtot = norm_g.shape[-1]
    dv = vtot // nh
    dk = (qkv.shape[-1] - vtot) // (2 * nh)
    L = MLSTM_CHUNK
    nc, nc_ctx = s // L, ctx // L
    assert s % L == 0 and ctx % L == 0 and dk % LANE == 0 and dv % LANE == 0 and o_off % dv == 0
    gt = gate_small[..., :4 * nh].reshape(b, nc, L, 4, nh).transpose(0, 4, 1, 3, 2)
    gt = jnp.pad(gt, ((0, 0), (0, 0), (0, 0), (0, 4), (0, 0)))
    gb = jnp.pad(gate_b.astype(f32).reshape(4, nh).T, ((0, 0), (0, 4)))
    gb = jnp.broadcast_to(gb[:, :, None], (nh, 8, L))
    ng = norm_g.astype(f32)[None, :]
    koff = nh
    voff = 2 * nh * dk // dv
    ooff = o_off // dv
    return pl.pallas_call(
        functools.partial(_mlstm_kernel, nc_ctx=nc_ctx, nc=nc, dk=dk, dv=dv),
        grid=(b, nh),
        in_specs=[
            pl.BlockSpec((1, s, dk), lambda bi, hi: (bi, 0, hi)),
            pl.BlockSpec((1, s, dk), lambda bi, hi: (bi, 0, koff + hi)),
            pl.BlockSpec((1, s, dv), lambda bi, hi: (bi, 0, voff + hi)),
            pl.BlockSpec((1, s, dv), lambda bi, hi: (bi, 0, ooff + hi)),
            pl.BlockSpec((1, 1, nc, 8, L), lambda bi, hi: (bi, hi, 0, 0, 0)),
            pl.BlockSpec((1, 8, L), lambda bi, hi: (hi, 0, 0)),
            pl.BlockSpec((1, dv), lambda bi, hi: (0, hi)),
        ],
        out_specs=pl.BlockSpec((1, s, dv), lambda bi, hi: (bi, 0, hi)),
        out_shape=jax.ShapeDtypeStruct((b, s, vtot), bf16),
        scratch_shapes=[pltpu.VMEM((s, dv), f32), pltpu.VMEM((dk, dv + LANE), f32)],
        compiler_params=_cparams(("parallel", "parallel")),
        name="mlstm_mixer",
    )(qkv, qkv, qkv, o_src, gt, gb, ng)


def _outproj_kernel(y_ref, w_ref, xl_ref, xc_ref, g_ref, ol_ref, oc_ref, *, n_ctx, cols, d):
    j = pl.program_id(1)
    acc = jnp.dot(y_ref[0], w_ref[...], preferred_element_type=f32)

    @pl.when(j < n_ctx)
    def _():
        oc_ref[0] = xc_ref[0] + g_ref[0, 0:1, :] * acc

    @pl.when(j >= n_ctx)
    def _():
        gl = g_ref[0, 1:2, :]
        if cols == 0:
            ol_ref[0] = xl_ref[0] + gl * acc
        else:
            nr = xl_ref.shape[1]
            for ci in range(cols):
                ol_ref[0, :, ci * d:(ci + 1) * d] = xl_ref[0, :, ci * d:(ci + 1) * d] + gl * acc[ci * nr:(ci + 1) * nr, :]


def outproj_residual(y, w, x_lat, x_ctx, gates, col_major):
    b, t, d = x_lat.shape
    ctx = x_ctx.shape[1]
    kin = y.shape[-1]
    tt = ROW_TILE
    n_ctx, n_lat = ctx // tt, t // tt
    if col_major:
        r = t // GRID_W
        cols = tt // r
        xl = x_lat.reshape(b, r, GRID_W * d)
        xl_spec = pl.BlockSpec((1, r, cols * d), lambda bi, j: (bi, 0, jnp.maximum(j - n_ctx, 0)))
    else:
        cols = 0
        xl = x_lat
        xl_spec = pl.BlockSpec((1, tt, d), lambda bi, j: (bi, jnp.maximum(j - n_ctx, 0), 0))
    ol, oc = pl.pallas_call(
        functools.partial(_outproj_kernel, n_ctx=n_ctx, cols=cols, d=d),
        grid=(b, n_ctx + n_lat),
        in_specs=[
            pl.BlockSpec((1, tt, kin), lambda bi, j: (bi, j, 0)),
            pl.BlockSpec((kin, d), lambda bi, j: (0, 0), pipeline_mode=pl.Buffered(1)),
            xl_spec,
            pl.BlockSpec((1, tt, d), lambda bi, j: (bi, jnp.minimum(j, n_ctx - 1), 0)),
            pl.BlockSpec((1, 2, d), lambda bi, j: (bi, 0, 0)),
        ],
        out_specs=[xl_spec, pl.BlockSpec((1, tt, d), lambda bi, j: (bi, jnp.minimum(j, n_ctx - 1), 0))],
        out_shape=[jax.ShapeDtypeStruct(xl.shape, f32), jax.ShapeDtypeStruct(x_ctx.shape, f32)],
        compiler_params=_cparams(("parallel", "arbitrary")),
        name="outproj_residual",
    )(y, w, xl, x_ctx, gates)
    return ol.reshape(b, t, d), oc


def _mod_kernel(c_ref, w_ref, b_ref, o_ref):
    cvec = c_ref[...]
    cs = (cvec * jax.nn.sigmoid(cvec)).astype(bf16)
    o_ref[0] = jnp.dot(cs, w_ref[0].astype(bf16), preferred_element_type=f32) + b_ref[0]


def modulation(c_rows, mod_w, mod_b):
    depth, d, n = mod_w.shape
    tn = 1024
    assert n % tn == 0
    return pl.pallas_call(
        _mod_kernel,
        grid=(depth, n // tn),
        in_specs=[
            pl.BlockSpec(c_rows.shape, lambda i, j: (0, 0)),
            pl.BlockSpec((1, d, tn), lambda i, j: (i, 0, j)),
            pl.BlockSpec((1, 1, tn), lambda i, j: (i, 0, j)),
        ],
        out_specs=pl.BlockSpec((1, c_rows.shape[0], tn), lambda i, j: (i, 0, j)),
        out_shape=jax.ShapeDtypeStruct((depth, c_rows.shape[0], n), f32),
        compiler_params=_cparams(("parallel", "parallel")),
        name="modulation",
    )(c_rows, mod_w, mod_b.reshape(depth, 1, n))


def _moe_kernel(be_ref, nu_ref, x_ref, wgu_ref, bgu_ref, wd_ref, bd_ref, o_ref, *, f):
    i = pl.program_id(0)
    dh = x_ref.shape[1]

    @pl.when(i < nu_ref[0])
    def _():
        xa, xb = _unpack_pair(x_ref[...])
        h = (jnp.dot(xa.astype(bf16), wgu_ref[0, :dh, :], preferred_element_type=f32)
             + jnp.dot(xb.astype(bf16), wgu_ref[0, dh:, :], preferred_element_type=f32) + bgu_ref[0])
        glu = jnp.minimum(h[:, :f], SWIGLU_LIMIT)
        lin = jnp.clip(h[:, f:], -SWIGLU_LIMIT, SWIGLU_LIMIT)
        act = glu * jax.nn.sigmoid(SWIGLU_ALPHA * glu) * (lin + 1.0)
        y = jnp.dot(act.astype(bf16), wd_ref[0], preferred_element_type=f32) + bd_ref[0]
        yb = y.astype(bf16).astype(f32)
        o_ref[...] = _pack_pair(yb[:, :dh], yb[:, dh:])

    @pl.when(i >= nu_ref[0])
    def _():
        o_ref[...] = jnp.zeros_like(o_ref)


def moe_experts(xs, block_e, n_used, gu_w, gu_b, down_w, down_b, tm):
    n_slots, dh = xs.shape
    e, d, f2 = gu_w.shape
    f = f2 // 2
    n_blocks = n_slots // tm
    last = lambda i, nu: jnp.minimum(i, nu[0] - 1)
    grid_spec = pltpu.PrefetchScalarGridSpec(
        num_scalar_prefetch=2,
        grid=(n_blocks,),
        in_specs=[
            pl.BlockSpec((tm, dh), lambda i, be, nu: (last(i, nu), 0)),
            pl.BlockSpec((1, d, f2), lambda i, be, nu: (be[last(i, nu)], 0, 0)),
            pl.BlockSpec((1, 1, f2), lambda i, be, nu: (be[last(i, nu)], 0, 0)),
            pl.BlockSpec((1, f, d), lambda i, be, nu: (be[last(i, nu)], 0, 0)),
            pl.BlockSpec((1, 1, d), lambda i, be, nu: (be[last(i, nu)], 0, 0)),
        ],
        out_specs=pl.BlockSpec((tm, dh), lambda i, be, nu: (i, 0)),
    )
    return pl.pallas_call(
        functools.partial(_moe_kernel, f=f),
        grid_spec=grid_spec,
        out_shape=jax.ShapeDtypeStruct((n_slots, dh), jnp.uint32),
        compiler_params=_cparams(("arbitrary",)),
        name="moe_experts",
    )(block_e, n_used, xs, gu_w, gu_b.reshape(e, 1, f2), down_w, down_b.reshape(e, 1, d))


ROUTE_TILE = 512


def _route_kernel(lg_ref, rb_ref, ri_ref, rg_ref, cnt_ref, run_ref, *, n_exp):
    i = pl.program_id(0)
    tb = lg_ref.shape[0]
    i32 = jnp.int32

    @pl.when(i == 0)
    def _():
        run_ref[...] = jnp.zeros_like(run_ref)

    lane = lax.broadcasted_iota(i32, (tb, LANE), 1)
    lg = jnp.where(lane < n_exp, lg_ref[...] + rb_ref[...], -jnp.inf)
    vals, idxs, hots = [], [], []
    for _ in range(TOP_K):
        m = jnp.max(lg, axis=1, keepdims=True)
        idx = jnp.min(jnp.where(lg == m, lane, LANE), axis=1, keepdims=True)
        hot = lane == idx
        lg = jnp.where(hot, -jnp.inf, lg)
        vals.append(m)
        idxs.append(idx)
        hots.append(hot)
    es = [jnp.exp(v - vals[0]) for v in vals]
    tot = es[0]
    for ev in es[1:]:
        tot = tot + ev
    msum = hots[0].astype(f32)
    for hot in hots[1:]:
        msum = msum + hot.astype(f32)
    tri = (lax.broadcasted_iota(i32, (tb, tb), 0) > lax.broadcasted_iota(i32, (tb, tb), 1)).astype(bf16)
    before = jnp.dot(tri, msum.astype(bf16), preferred_element_type=f32) + run_ref[...]
    ri = jnp.zeros((tb, LANE), i32)
    rg = jnp.zeros((tb, LANE), f32)
    for k in range(TOP_K):
        rank = jnp.sum(jnp.where(hots[k], before, 0.0), axis=1, keepdims=True)
        ri = jnp.where(lane == k, idxs[k], ri)
        ri = jnp.where(lane == TOP_K + k, rank.astype(i32), ri)
        rg = jnp.where(lane == k, es[k] / tot, rg)
    ri_ref[...] = ri
    rg_ref[...] = rg
    run_ref[...] = run_ref[...] + jnp.sum(msum, axis=0, keepdims=True)
    cnt_ref[...] = run_ref[...]


def route(logits, router_b, n_exp):
    n = logits.shape[0]
    tb = _pick_tile(n, (ROUTE_TILE, 256))
    return pl.pallas_call(
        functools.partial(_route_kernel, n_exp=n_exp),
        grid=(n // tb,),
        in_specs=[pl.BlockSpec((tb, LANE), lambda i: (i, 0)), pl.BlockSpec((1, LANE), lambda i: (0, 0))],
        out_specs=[pl.BlockSpec((tb, LANE), lambda i: (i, 0)), pl.BlockSpec((tb, LANE), lambda i: (i, 0)),
                   pl.BlockSpec((1, LANE), lambda i: (0, 0))],
        out_shape=[jax.ShapeDtypeStruct((n, LANE), jnp.int32), jax.ShapeDtypeStruct((n, LANE), f32),
                   jax.ShapeDtypeStruct((1, LANE), f32)],
        scratch_shapes=[pltpu.VMEM((1, LANE), f32)],
        compiler_params=_cparams(("arbitrary",)),
        name="route",
    )(logits, router_b)


def _row_copy(src, s_row, dst, d_row, sem):
    return pltpu.make_async_copy(src.at[pl.ds(s_row, 1)], dst.at[pl.ds(d_row, 1)], sem)


def _dispatch_kernel(zs_ref, zc_ref, nu_ref, slot_ref, u_ref, xs_ref, zero_ref, sem, *, n_exp, tm):
    i = pl.program_id(0)
    tb = u_ref.shape[0]
    n_blocks = xs_ref.shape[0] // tm

    @pl.when(i == 0)
    def _():
        zero_ref[...] = jnp.zeros_like(zero_ref)

        def block_copy(blk):
            return pltpu.make_async_copy(zero_ref, xs_ref.at[pl.ds(pl.multiple_of(blk * tm, tm), tm)], sem)

        def start_block(blk, carry):
            block_copy(blk).start()
            return carry

        def wait_block(blk, carry):
            block_copy(blk).wait()
            return carry

        lax.fori_loop(nu_ref[0], n_blocks, start_block, 0)
        lax.fori_loop(nu_ref[0], n_blocks, wait_block, 0)

        def per_expert(e, carry):
            def start(r, c):
                _row_copy(zero_ref, 0, xs_ref, zs_ref[e] + r, sem).start()
                return c

            def wait(r, c):
                _row_copy(zero_ref, 0, xs_ref, 0, sem).wait()
                return c

            lax.fori_loop(0, zc_ref[e], start, 0)
            lax.fori_loop(0, zc_ref[e], wait, 0)
            return carry

        lax.fori_loop(0, n_exp, per_expert, 0)

    def start(r, c):
        for k in range(TOP_K):
            _row_copy(u_ref, r, xs_ref, slot_ref[0, 0, r * TOP_K + k], sem).start()
        return c

    def wait(r, c):
        _row_copy(u_ref, 0, xs_ref, 0, sem).wait()
        return c

    lax.fori_loop(0, tb, start, 0, unroll=4)
    lax.fori_loop(0, tb * TOP_K, wait, 0, unroll=8)


def dispatch(u, slot, zero_start, zero_count, n_used, n_blocks, tm):
    n, w = u.shape
    tb = ROW_TILE
    n_exp = zero_start.shape[0]
    grid_spec = pltpu.PrefetchScalarGridSpec(
        num_scalar_prefetch=3,
        grid=(n // tb,),
        in_specs=[
            pl.BlockSpec((1, 1, tb * TOP_K), lambda i, zs, zc, nu: (i, 0, 0), memory_space=pltpu.SMEM),
            pl.BlockSpec((tb, w), lambda i, zs, zc, nu: (i, 0)),
        ],
        out_specs=pl.BlockSpec(memory_space=pl.ANY),
        scratch_shapes=[pltpu.VMEM((tm, w), jnp.uint32), pltpu.SemaphoreType.DMA],
    )
    return pl.pallas_call(
        functools.partial(_dispatch_kernel, n_exp=n_exp, tm=tm),
        grid_spec=grid_spec,
        out_shape=jax.ShapeDtypeStruct((n_blocks * tm, w), jnp.uint32),
        compiler_params=_cparams(("arbitrary",)),
        name="dispatch",
    )(zero_start, zero_count, n_used, slot.reshape(n // tb, 1, tb * TOP_K), u)


def _combine_kernel(slot_ref, rg_ref, xl_ref, xc_ref, g_ref, ys_ref, ol_ref, oc_ref, buf_ref, sem, *, n_ctx):
    j = pl.program_id(1)
    tb = rg_ref.shape[0]

    def start(r, c):
        for k in range(TOP_K):
            pltpu.make_async_copy(ys_ref.at[pl.ds(slot_ref[0, 0, r * TOP_K + k], 1)],
                                  buf_ref.at[k, pl.ds(r, 1)], sem).start()
        return c

    def wait(r, c):
        pltpu.make_async_copy(ys_ref.at[pl.ds(0, 1)], buf_ref.at[0, pl.ds(0, 1)], sem).wait()
        return c

    lax.fori_loop(0, tb, start, 0, unroll=4)
    lax.fori_loop(0, tb * TOP_K, wait, 0, unroll=8)

    gates = rg_ref[...]
    acc = None
    for k in range(TOP_K):
        ya, yb = _unpack_pair(buf_ref[k])
        term = gates[:, k:k + 1] * jnp.concatenate([ya, yb], axis=1)
        acc = term if acc is None else acc + term

    @pl.when(j < n_ctx)
    def _():
        oc_ref[0] = xc_ref[0] + g_ref[0, 0:1, :] * acc

    @pl.when(j >= n_ctx)
    def _():
        ol_ref[0] = xl_ref[0] + g_ref[0, 1:2, :] * acc


def combine(ys, slot, rg, x_lat, x_ctx, gates):
    b, t, d = x_lat.shape
    ctx = x_ctx.shape[1]
    tb = ROW_TILE
    n_ctx, n_lat = ctx // tb, t // tb
    steps = n_ctx + n_lat
    return pl.pallas_call(
        functools.partial(_combine_kernel, n_ctx=n_ctx),
        grid=(b, steps),
        in_specs=[
            pl.BlockSpec((1, 1, tb * TOP_K), lambda bi, j: (bi * steps + j, 0, 0), memory_space=pltpu.SMEM),
            pl.BlockSpec((tb, LANE), lambda bi, j: (bi * steps + j, 0)),
            pl.BlockSpec((1, tb, d), lambda bi, j: (bi, jnp.maximum(j - n_ctx, 0), 0)),
            pl.BlockSpec((1, tb, d), lambda bi, j: (bi, jnp.minimum(j, n_ctx - 1), 0)),
            pl.BlockSpec((1, 2, d), lambda bi, j: (bi, 0, 0)),
            pl.BlockSpec(memory_space=pl.ANY),
        ],
        out_specs=[pl.BlockSpec((1, tb, d), lambda bi, j: (bi, jnp.maximum(j - n_ctx, 0), 0)),
                   pl.BlockSpec((1, tb, d), lambda bi, j: (bi, jnp.minimum(j, n_ctx - 1), 0))],
        out_shape=[jax.ShapeDtypeStruct(x_lat.shape, f32), jax.ShapeDtypeStruct(x_ctx.shape, f32)],
        scratch_shapes=[pltpu.VMEM((TOP_K, tb, d // 2), jnp.uint32), pltpu.SemaphoreType.DMA],
        compiler_params=_cparams(("arbitrary", "arbitrary")),
        name="combine",
    )(slot.reshape(b * steps, 1, tb * TOP_K), rg, x_lat, x_ctx, gates, ys)


def _final_norm_kernel(x_ref, g_ref, o_ref):
    x = x_ref[0]
    o_ref[0] = (x * lax.rsqrt(jnp.mean(x * x, axis=-1, keepdims=True) + NORM_EPS)) * g_ref[...]


def final_norm(x, g):
    b, t, d = x.shape
    tt = 512 if t % 512 == 0 else ROW_TILE
    return pl.pallas_call(
        _final_norm_kernel,
        grid=(b, t // tt),
        in_specs=[pl.BlockSpec((1, tt, d), lambda bi, j: (bi, j, 0)), pl.BlockSpec((1, d), lambda bi, j: (0, 0))],
        out_specs=pl.BlockSpec((1, tt, d), lambda bi, j: (bi, j, 0)),
        out_shape=jax.ShapeDtypeStruct((b, t, d), f32),
        compiler_params=_cparams(("parallel", "parallel")),
        name="final_norm",
    )(x, g)


MOE_TILE = 512


def _pick_tile(n, prefs):
    for p in prefs:
        if n % p == 0:
            return p
    raise ValueError(f"no tile in {prefs} divides {n}")


def _pad_lanes(w):
    return jnp.pad(w, ((0, 0), (0, LANE - w.shape[1])))


def _project(h, w_main):
    b, s, d = h.shape
    m = b * s
    n = w_main.shape[1]
    out = matmul(h.reshape(m, d), w_main, bf16, _pick_tile(m, (1024, 512, 256)), _pick_tile(n, (1024, 512, 256, 128)))
    return out.reshape(b, s, n)


def _moe_ffn(u, logits, router_b, gu_w, gu_b, down_w, down_b, x_lat, x_ctx, gates):
    n = u.shape[0]
    e = gu_w.shape[0]
    tm = MOE_TILE
    i32 = jnp.int32
    ri, rg, counts = route(logits, _pad_lanes(router_b.astype(f32)[None, :]), e)
    cnt = counts[0, :e].astype(i32)
    padded = (cnt + tm - 1) // tm * tm
    pad_end = jnp.cumsum(padded)
    pad_start = pad_end - padded
    n_blocks = -(-(n * TOP_K) // tm) + e
    top_idx, rank = ri[:, :TOP_K], ri[:, TOP_K:2 * TOP_K]
    hot = top_idx[:, :, None] == jnp.arange(e, dtype=i32)[None, None, :]
    slot = (jnp.sum(jnp.where(hot, pad_start[None, None, :], 0), axis=-1) + rank).reshape(-1)
    block_start = jnp.arange(n_blocks, dtype=i32) * tm
    block_e = jnp.minimum(jnp.sum((block_start[:, None] >= pad_end[None, :]).astype(i32), axis=1), e - 1)
    n_used = (pad_end[-1] // tm).astype(i32).reshape(1)
    xs = dispatch(u, slot, pad_start + cnt, padded - cnt, n_used, n_blocks, tm)
    ys = moe_experts(xs, block_e, n_used, gu_w, gu_b, down_w, down_b, tm)
    return combine(ys, slot, rg, x_lat, x_ctx, gates)


def kernel(x, c, ctx, c_ctx, mod_w, mod_b, norm1_g, norm2_g, final_g, m_in_w, m_conv_w, m_conv_b, m_dt_bias, m_a_log, m_d, m_norm_g, m_out_w, l_in_w, l_conv_w, l_conv_b, l_gate_b, l_norm_g, l_out_w, router_w, router_b, e_gu_w, e_gu_b, e_down_w, e_down_b):
    b, t, d = x.shape
    n_ctx_tok = ctx.shape[1]
    depth = mod_w.shape[0]
    n_exp = router_w.shape[-1]
    m_inner = m_out_w.shape[1]
    m_conv = m_conv_w.shape[-1]
    l_qkv = l_conv_w.shape[-1]
    l_vtot = l_out_w.shape[1]

    c_rows = jnp.zeros((16, d), f32).at[:b].set(c).at[b].set(c_ctx)
    mod_all = modulation(c_rows, mod_w, mod_b)

    xl, xc = x, ctx
    for i in range(depth):
        j = i // 2
        mod = mod_all[i].reshape(16, 6, d)
        lat, cx = mod[:b], jnp.broadcast_to(mod[b][None], (b, 6, d))

        def mods(k):
            return jnp.stack([cx[:, k], cx[:, k + 1], lat[:, k], lat[:, k + 1]], axis=1)

        def gate(k):
            return jnp.stack([cx[:, k], lat[:, k]], axis=1)

        if i % 2 == 0:
            w_in = m_in_w[j].astype(bf16)
            n_main = m_inner + m_conv
            h, dt_small = normmod(xl, xc, norm1_g[i][None], mods(0), _pad_lanes(w_in[:, n_main:]), False)
            proj = _project(h, w_in[:, :n_main])
            xbc = conv_silu(proj, m_conv_w[j], m_conv_b[j][None], m_inner, 512, n_ctx_tok)
            y = ssd_mixer(xbc, proj, dt_small, m_dt_bias[j], m_a_log[j], m_d[j], m_norm_g[j], n_ctx_tok)
            xl, xc = outproj_residual(y, m_out_w[j].astype(bf16), xl, xc, gate(2), False)
        else:
            w_in = l_in_w[j].astype(bf16)
            n_main = l_qkv + l_vtot
            h, gate_small = normmod(xl, xc, norm1_g[i][None], mods(0), _pad_lanes(w_in[:, n_main:]), True)
            proj = _project(h, w_in[:, :n_main])
            qkv = conv_silu(proj, l_conv_w[j], l_conv_b[j][None], 0, 512, n_ctx_tok)
            y = mlstm_mixer(qkv, proj, l_qkv, gate_small, l_gate_b[j], l_norm_g[j], n_ctx_tok)
            xl, xc = outproj_residual(y, l_out_w[j].astype(bf16), xl, xc, gate(2), True)

        u, logits = normmod(xl, xc, norm2_g[i][None], mods(3), _pad_lanes(router_w[i].astype(bf16)), False, pack=True)
        s = n_ctx_tok + t
        xl, xc = _moe_ffn(u.reshape(b * s, d // 2), logits.reshape(b * s, LANE), router_b[i], e_gu_w[i].astype(bf16),
                          e_gu_b[i], e_down_w[i].astype(bf16), e_down_b[i], xl, xc, gate(5))
    return final_norm(xl, final_g[None])
```

```python
import functools

import jax
import jax.numpy as jnp
from jax import lax
from jax.experimental import pallas as pl
from jax.experimental.pallas import tpu as pltpu

f32 = jnp.float32
bf16 = jnp.bfloat16

NORM_EPS = 1e-6
CONV_W = 5
GRID_W = 64
M_D_STATE = 128
TOP_K = 4
SWIGLU_LIMIT = 7.0
SWIGLU_ALPHA = 1.702
SSD_CHUNK = 128
MLSTM_CHUNK = 256
ROW_TILE = 256
LANE = 128
V7X_VMEM_LIMIT = 60000 * 1024


def _cparams(sem):
    return pltpu.CompilerParams(dimension_semantics=sem, vmem_limit_bytes=V7X_VMEM_LIMIT)


def _pack_pair(a, b):
    u32 = jnp.uint32
    return (lax.bitcast_convert_type(a, u32) & u32(0xFFFF0000)) | (lax.bitcast_convert_type(b, u32) >> 16)


def _unpack_pair(w):
    u32 = jnp.uint32
    return (lax.bitcast_convert_type(w & u32(0xFFFF0000), f32), lax.bitcast_convert_type(w << 16, f32))


def _normmod_kernel(xl_ref, xc_ref, g_ref, mod_ref, ws_ref, h_ref, small_ref, *, n_ctx, cols, d, pack):
    j = pl.program_id(1)

    def rows(x, shift, scale, r0, nr):
        ms = jnp.mean(x * x, axis=-1, keepdims=True)
        y = (x * lax.rsqrt(ms + NORM_EPS)) * g_ref[...]
        hb = (y * (1.0 + scale) + shift).astype(bf16)
        if pack:
            hf = hb.astype(f32)
            h_ref[0, r0:r0 + nr, :] = _pack_pair(hf[:, :d // 2], hf[:, d // 2:])
        else:
            h_ref[0, r0:r0 + nr, :] = hb
        small_ref[0, r0:r0 + nr, :] = jnp.dot(hb, ws_ref[...], preferred_element_type=f32)

    @pl.when(j < n_ctx)
    def _():
        rows(xc_ref[0], mod_ref[0, 0:1, :], mod_ref[0, 1:2, :], 0, xc_ref.shape[1])

    @pl.when(j >= n_ctx)
    def _():
        if cols == 0:
            rows(xl_ref[0], mod_ref[0, 2:3, :], mod_ref[0, 3:4, :], 0, xl_ref.shape[1])
        else:
            nr = xl_ref.shape[1]
            for ci in range(cols):
                rows(xl_ref[0, :, ci * d:(ci + 1) * d], mod_ref[0, 2:3, :], mod_ref[0, 3:4, :], ci * nr, nr)


def normmod(x_lat, x_ctx, g, mods, w_small, col_major, pack=False):
    b, t, d = x_lat.shape
    ctx = x_ctx.shape[1]
    tt = ROW_TILE
    n_ctx, n_lat = ctx // tt, t // tt
    assert ctx % tt == 0 and t % tt == 0
    if col_major:
        r = t // GRID_W
        cols = tt // r
        assert tt % r == 0 and GRID_W % cols == 0
        xl = x_lat.reshape(b, r, GRID_W * d)
        xl_spec = pl.BlockSpec((1, r, cols * d), lambda bi, j: (bi, 0, jnp.maximum(j - n_ctx, 0)))
    else:
        cols = 0
        xl = x_lat
        xl_spec = pl.BlockSpec((1, tt, d), lambda bi, j: (bi, jnp.maximum(j - n_ctx, 0), 0))
    s = ctx + t
    dh = d // 2 if pack else d
    return pl.pallas_call(
        functools.partial(_normmod_kernel, n_ctx=n_ctx, cols=cols, d=d, pack=pack),
        grid=(b, n_ctx + n_lat),
        in_specs=[
            xl_spec,
            pl.BlockSpec((1, tt, d), lambda bi, j: (bi, jnp.minimum(j, n_ctx - 1), 0)),
            pl.BlockSpec((1, d), lambda bi, j: (0, 0)),
            pl.BlockSpec((1, 4, d), lambda bi, j: (bi, 0, 0)),
            pl.BlockSpec((d, LANE), lambda bi, j: (0, 0)),
        ],
        out_specs=[
            pl.BlockSpec((1, tt, dh), lambda bi, j: (bi, j, 0)),
            pl.BlockSpec((1, tt, LANE), lambda bi, j: (bi, j, 0)),
        ],
        out_shape=[jax.ShapeDtypeStruct((b, s, dh), jnp.uint32 if pack else bf16),
                   jax.ShapeDtypeStruct((b, s, LANE), f32)],
        compiler_params=_cparams(("parallel", "arbitrary")),
        name="normmod",
    )(xl, x_ctx, g, mods, w_small)


def _mm_kernel(x_ref, w_ref, o_ref):
    o_ref[...] = jnp.dot(x_ref[...], w_ref[...], preferred_element_type=f32).astype(o_ref.dtype)


def matmul(x, w, out_dtype, tm, tn):
    m, k = x.shape
    n = w.shape[1]
    assert m % tm == 0 and n % tn == 0
    return pl.pallas_call(
        _mm_kernel,
        grid=(m // tm, n // tn),
        in_specs=[pl.BlockSpec((tm, k), lambda i, j: (i, 0)), pl.BlockSpec((k, tn), lambda i, j: (0, j))],
        out_specs=pl.BlockSpec((tm, tn), lambda i, j: (i, j)),
        out_shape=jax.ShapeDtypeStruct((m, n), out_dtype),
        compiler_params=_cparams(("parallel", "arbitrary")),
        name="matmul",
    )(x, w)


def _conv_kernel(u_ref, w_ref, b_ref, o_ref, pad_ref, *, ctx, tile):
    s, cb = u_ref.shape[1], u_ref.shape[2]
    half = CONV_W // 2
    zero8 = jnp.zeros((8, cb), f32)
    pad_ref[0:8, :] = zero8
    pad_ref[8 + ctx:16 + ctx, :] = zero8
    pad_ref[16 + s:24 + s, :] = zero8
    for r0 in range(0, s, tile):
        base = 8 if r0 < ctx else 16
        pad_ref[base + r0:base + r0 + tile, :] = u_ref[0, r0:r0 + tile, :].astype(f32)
    w = w_ref[...]
    bias = b_ref[...]
    for r0 in range(0, s, tile):
        base = 8 if r0 < ctx else 16
        acc = bias + w[0:1, :] * pad_ref[base + r0 - half:base + r0 - half + tile, :]
        for k in range(1, CONV_W):
            acc = acc + w[k:k + 1, :] * pad_ref[base + r0 - half + k:base + r0 - half + k + tile, :]
        o_ref[0, r0:r0 + tile, :] = (acc * jax.nn.sigmoid(acc)).astype(o_ref.dtype)


def conv_silu(u, w, bias, lane_off, cb, ctx):
    b, s, _ = u.shape
    c = w.shape[1]
    assert c % cb == 0 and lane_off % cb == 0
    tile = ROW_TILE
    assert ctx % tile == 0 and s % tile == 0
    off = lane_off // cb
    return pl.pallas_call(
        functools.partial(_conv_kernel, ctx=ctx, tile=tile),
        grid=(b, c // cb),
        in_specs=[
            pl.BlockSpec((1, s, cb), lambda bi, j: (bi, 0, off + j)),
            pl.BlockSpec((CONV_W, cb), lambda bi, j: (0, j)),
            pl.BlockSpec((1, cb), lambda bi, j: (0, j)),
        ],
        out_specs=pl.BlockSpec((1, s, cb), lambda bi, j: (bi, 0, j)),
        out_shape=jax.ShapeDtypeStruct((b, s, c), bf16),
        scratch_shapes=[pltpu.VMEM((s + 24, cb), f32)],
        compiler_params=_cparams(("parallel", "parallel")),
        name="conv_silu",
    )(u, w, bias)


def _split3(a):
    hi = a.astype(bf16)
    r1 = a - hi.astype(f32)
    mid = r1.astype(bf16)
    lo = (r1 - mid.astype(f32)).astype(bf16)
    return jnp.concatenate([hi, mid, lo], axis=0)


def _softplus(x):
    return jnp.maximum(x, 0.0) + jnp.log1p(jnp.exp(-jnp.abs(x)))


def _dot_t(a, b):
    return lax.dot_general(a, b, (((0,), (0,)), ((), ())), preferred_element_type=f32)


def _dot_nt(a, b):
    return lax.dot_general(a, b, (((1,), (1,)), ((), ())), preferred_element_type=f32)


def _chunk_transpose_kernel(x_ref, o_ref):
    o_ref[0, 0] = x_ref[0].T


def chunk_transpose(x, chunk):
    b, s, w = x.shape
    nc = s // chunk
    return pl.pallas_call(
        _chunk_transpose_kernel,
        grid=(b, nc),
        in_specs=[pl.BlockSpec((1, chunk, w), lambda bi, ci: (bi, ci, 0))],
        out_specs=pl.BlockSpec((1, 1, w, chunk), lambda bi, ci: (bi, ci, 0, 0)),
        out_shape=jax.ShapeDtypeStruct((b, nc, w, chunk), f32),
        compiler_params=_cparams(("parallel", "parallel")),
        name="chunk_transpose",
    )(x)


def _hi_mid(v):
    hi = v.astype(bf16)
    return jnp.concatenate([hi, (v - hi.astype(f32)).astype(bf16)], axis=1)


def _ssd_kernel(x_ref, b_ref, c_ref, z_ref, dt_ref, bias_ref, a_ref, dtc_ref, biasc_ref, ac_ref, dskip_ref, ng_ref,
                o_ref, yf_ref, h_ref, earg_ref, echan_ref, *, nc_ctx, nc, heads, headdim, n_heads):
    L = SSD_CHUNK
    R, P = heads, headdim
    RP = R * P
    i32 = jnp.int32
    gi = pl.program_id(1)

    row = lax.broadcasted_iota(i32, (L, L), 0)
    col = lax.broadcasted_iota(i32, (L, L), 1)
    lane_lo = lax.broadcasted_iota(i32, (L, 2 * P), 1) < P
    tri_c = jnp.tile((col <= row).astype(bf16), (1, 3))

    for d in range(2):
        first = d * n_heads + gi * R
        er = lax.broadcasted_iota(i32, (2 * LANE, R * L), 0) % LANE
        ec = lax.broadcasted_iota(i32, (2 * LANE, R * L), 1) // L
        earg_ref[d] = (er == first + ec).astype(bf16)
        fr = lax.broadcasted_iota(i32, (2 * LANE, RP), 0) % LANE
        fc = lax.broadcasted_iota(i32, (2 * LANE, RP), 1) // P
        echan_ref[d] = (fr == first + fc).astype(bf16)

    def chunk(ci, d):
        fwd = d == 0
        tri = ((row <= col) if fwd else (row >= col)).astype(bf16)
        mask = (col <= row) if fwd else (col >= row)
        r0 = pl.multiple_of(ci * L, L)
        raw = dt_ref[0, ci, pl.ds(pl.multiple_of(d * n_heads + gi * R, 8), R), :]
        dt = _softplus(raw + bias_ref[0, d * R:(d + 1) * R, :])
        a = dt * a_ref[0, d * R:(d + 1) * R, :]
        cum3 = jnp.dot(_split3(a), tri, preferred_element_type=f32)
        cum = cum3[0:R] + cum3[R:2 * R] + cum3[2 * R:3 * R]
        srow = cum - jnp.log(jnp.maximum(dt, 1e-37))
        crow = jnp.concatenate([srow[h:h + 1, :] for h in range(R)], axis=1)
        dtc = _softplus(dtc_ref[0, pl.ds(r0, L), :] + biasc_ref[...])
        ac = dtc * ac_ref[...]
        pre = jnp.dot(tri_c, _split3(ac), preferred_element_type=f32)
        tot = pre[L - 1:L, :]
        cumc = pre if fwd else tot - pre + ac
        arg = jnp.dot(_hi_mid(cumc), earg_ref[d], preferred_element_type=f32) - crow
        ecum_x = jnp.dot(_hi_mid(jnp.exp(cumc)), echan_ref[d], preferred_element_type=f32)
        ws_x = jnp.dot(_hi_mid(jnp.exp(tot - cumc) * dtc), echan_ref[d], preferred_element_type=f32)
        etot_x = ecum_x[L - 1:L, :] if fwd else ecum_x[0:1, :]

        xc = x_ref[0, pl.ds(r0, L), :]
        bc = b_ref[0, pl.ds(r0, L), :]
        cc = c_ref[0, pl.ds(r0, L), :]
        cb = _dot_nt(cc, bc)
        ys = []
        for p in range(R // 2):
            w_pair = []
            for h in (2 * p, 2 * p + 1):
                dec = jnp.exp(jnp.where(mask, arg[:, h * L:(h + 1) * L], -jnp.inf))
                w_pair.append((dec * cb).astype(bf16))
            xp = xc[:, 2 * p * P:2 * (p + 1) * P]
            zero = jnp.zeros_like(xp)
            rhs = jnp.concatenate([jnp.where(lane_lo, xp, zero), jnp.where(lane_lo, zero, xp)], axis=0)
            ys.append(jnp.dot(jnp.concatenate(w_pair, axis=1), rhs, preferred_element_type=f32))
        y_intra = jnp.concatenate(ys, axis=1)
        ht = h_ref[d]
        y = y_intra + jnp.dot(cc, ht.astype(bf16), preferred_element_type=f32) * ecum_x
        xw = (xc.astype(f32) * ws_x).astype(bf16)
        h_ref[d] = ht * etot_x + _dot_t(bc, xw)
        return r0, y

    h_ref[...] = jnp.zeros_like(h_ref)
    yf_ref[...] = jnp.zeros_like(yf_ref)

    def scan_body(s, carry):
        r0, y = chunk(s, 0)
        yf_ref[pl.ds(r0, L), :] += y
        r0, y = chunk(jnp.where(s < nc_ctx, nc_ctx - 1 - s, nc + nc_ctx - 1 - s), 1)
        yf_ref[pl.ds(r0, L), :] += y
        return carry

    lax.fori_loop(0, nc, scan_body, 0)

    def finish_body(s, carry):
        r0 = pl.multiple_of(s * L, L)
        y = yf_ref[pl.ds(r0, L), :] + dskip_ref[...] * x_ref[0, pl.ds(r0, L), :].astype(f32)
        z = z_ref[0, pl.ds(r0, L), :].astype(f32)
        y = y * (z * jax.nn.sigmoid(z))
        y = y * lax.rsqrt(jnp.mean(y * y, axis=-1, keepdims=True) + NORM_EPS)
        o_ref[0, pl.ds(r0, L), :] = (y * ng_ref[...]).astype(o_ref.dtype)
        return carry

    lax.fori_loop(0, nc, finish_body, 0)


def ssd_mixer(xbc, z_src, dt_small, dt_bias, a_log, d_skip, norm_g, ctx):
    b, s, _ = xbc.shape
    n = M_D_STATE
    nh = a_log.shape[-1]
    inner = norm_g.shape[-1]
    p = inner // nh
    g = (xbc.shape[-1] - inner) // (2 * n)
    r = nh // g
    rp = r * p
    L = SSD_CHUNK
    nc, nc_ctx = s // L, ctx // L
    assert p * 2 == LANE and s % L == 0 and ctx % L == 0 and r % 8 == 0 and rp % LANE == 0
    dt_t = chunk_transpose(dt_small, L)

    def per_head(v):
        v = v.astype(f32).reshape(2, g, r).transpose(1, 0, 2).reshape(g, 2 * r, 1)
        return jnp.broadcast_to(v, (g, 2 * r, L))

    bias_t = per_head(dt_bias)
    neg_a = -jnp.exp(a_log.astype(f32))
    a_t = per_head(neg_a)
    assert 2 * nh <= LANE and dt_small.shape[-1] == LANE
    bias_c = jnp.pad(dt_bias.astype(f32).reshape(1, 2 * nh), ((0, 0), (0, LANE - 2 * nh)))
    a_c = jnp.pad(neg_a.reshape(1, 2 * nh), ((0, 0), (0, LANE - 2 * nh)))
    dskip = jnp.repeat(d_skip.astype(f32), p)[None, :]
    ng = norm_g.astype(f32)[None, :]
    boff = inner // n
    return pl.pallas_call(
        functools.partial(_ssd_kernel, nc_ctx=nc_ctx, nc=nc, heads=r, headdim=p, n_heads=nh),
        grid=(b, g),
        in_specs=[
            pl.BlockSpec((1, s, rp), lambda bi, gi: (bi, 0, gi)),
            pl.BlockSpec((1, s, n), lambda bi, gi: (bi, 0, boff + gi)),
            pl.BlockSpec((1, s, n), lambda bi, gi: (bi, 0, boff + g + gi)),
            pl.BlockSpec((1, s, rp), lambda bi, gi: (bi, 0, gi)),
            pl.BlockSpec((1, nc, LANE, L), lambda bi, gi: (bi, 0, 0, 0)),
            pl.BlockSpec((1, 2 * r, L), lambda bi, gi: (gi, 0, 0)),
            pl.BlockSpec((1, 2 * r, L), lambda bi, gi: (gi, 0, 0)),
            pl.BlockSpec((1, s, LANE), lambda bi, gi: (bi, 0, 0)),
            pl.BlockSpec((1, LANE), lambda bi, gi: (0, 0)),
            pl.BlockSpec((1, LANE), lambda bi, gi: (0, 0)),
            pl.BlockSpec((1, rp), lambda bi, gi: (0, gi)),
            pl.BlockSpec((1, rp), lambda bi, gi: (0, gi)),
        ],
        out_specs=pl.BlockSpec((1, s, rp), lambda bi, gi: (bi, 0, gi)),
        out_shape=jax.ShapeDtypeStruct((b, s, inner), bf16),
        scratch_shapes=[pltpu.VMEM((s, rp), f32), pltpu.VMEM((2, n, rp), f32),
                        pltpu.VMEM((2, 2 * LANE, r * L), bf16), pltpu.VMEM((2, 2 * LANE, rp), bf16)],
        compiler_params=_cparams(("parallel", "parallel")),
        name="ssd_mixer",
    )(xbc, xbc, xbc, z_src, dt_t, bias_t, a_t, dt_small, bias_c, a_c, dskip, ng)


def _stack_rows(rows, n):
    w = next(r for r in rows if r is not None).shape[1]
    rid = lax.broadcasted_iota(jnp.int32, (n, w), 0)
    out = jnp.zeros((n, w), f32)
    for i, r in enumerate(rows):
        if r is not None:
            out = jnp.where(rid == i, jnp.broadcast_to(r, (n, w)), out)
    return out


def _parts3(a):
    hi = a.astype(bf16).astype(f32)
    mid = (a - hi).astype(bf16).astype(f32)
    lo = a - hi - mid
    return [hi, mid, lo]


def _mlstm_kernel(q_ref, k_ref, v_ref, o_ref, g_ref, gb_ref, gc_ref, gbc_ref, ng_ref, out_ref, hf_ref, c_ref,
                  *, nc_ctx, nc, dk, dv, n_heads):
    L = MLSTM_CHUNK
    i32 = jnp.int32
    scale = dk ** -0.5
    hi_ = pl.program_id(1)
    row = lax.broadcasted_iota(i32, (L, L), 0)
    col = lax.broadcasted_iota(i32, (L, L), 1)
    lane_e = lax.broadcasted_iota(i32, (L, LANE), 1)
    tri_c = jnp.tile((col <= row).astype(bf16), (1, 3))

    def chunk(ci, d, m):
        fwd = d == 0
        tri = ((row <= col) if fwd else (row >= col)).astype(bf16)
        mask = (col <= row) if fwd else (col >= row)
        r0 = pl.multiple_of(ci * L, L)
        lane_i = (2 * d) * n_heads + hi_
        ip = g_ref[0, ci, pl.ds(lane_i, 1), :] + gb_ref[0, 2 * d:2 * d + 1, :]
        logf = -_softplus(-(g_ref[0, ci, pl.ds(lane_i + n_heads, 1), :] + gb_ref[0, 2 * d + 1:2 * d + 2, :]))
        cum3 = jnp.dot(_stack_rows(_parts3(logf), 16).astype(bf16), tri, preferred_element_type=f32)
        bb = cum3[0:1] + cum3[1:2] + cum3[2:3]
        btot = bb[:, L - 1:L] if fwd else bb[:, 0:1]
        u = ip - bb
        gcol = gc_ref[0, pl.ds(r0, L), :] + gbc_ref[...]
        lfc = -_softplus(-gcol)
        pre = jnp.dot(tri_c, _split3(lfc), preferred_element_type=f32)
        cumc = pre if fwd else pre[L - 1:L, :] - pre + lfc
        b_col = jnp.sum(jnp.where(lane_e == lane_i + n_heads, cumc, 0.0), axis=1, keepdims=True)
        u_col = jnp.sum(jnp.where(lane_e == lane_i, gcol, 0.0), axis=1, keepdims=True) - b_col
        dm = jnp.where(mask, b_col + u, -jnp.inf)
        inter = b_col + m
        m_out = jnp.maximum(inter, jnp.max(dm, axis=1, keepdims=True))
        q = q_ref[0, pl.ds(r0, L), :]
        k = k_ref[0, pl.ds(r0, L), :]
        v = v_ref[0, pl.ds(r0, L), :]
        w = jnp.exp(dm - m_out) * (_dot_nt(q, k) * scale)
        inter_w = jnp.exp(inter - m_out)
        cx = c_ref[d]
        qc = jnp.dot(q, cx.astype(bf16), preferred_element_type=f32) * scale
        num = jnp.dot(w.astype(bf16), v, preferred_element_type=f32) + inter_w * qc[:, :dv]
        den = jnp.sum(w, axis=1, keepdims=True) + inter_w * (qc[:, dv:dv + 1] + qc[:, dv + 1:dv + 2])
        hk = num / jnp.maximum(jnp.abs(den), jnp.exp(-m_out))
        m_new = jnp.maximum(btot + m, btot + jnp.max(u, axis=1, keepdims=True))
        e_col = jnp.exp(u_col + btot - m_new)
        e_hi = e_col.astype(bf16).astype(f32)
        e_ext = jnp.where(lane_e == 0, e_hi, jnp.where(lane_e == 1, e_col - e_hi, 0.0))
        vext = jnp.concatenate([v.astype(f32) * e_col, e_ext], axis=1).astype(bf16)
        c_ref[d] = jnp.exp(btot + m - m_new) * cx + _dot_t(k, vext)
        return r0, hk, m_new

    m0 = jnp.full((1, 1), -jnp.inf, f32)
    c_ref[...] = jnp.zeros_like(c_ref)
    hf_ref[...] = jnp.zeros_like(hf_ref)

    def scan_body(s, ms):
        r0, hk, mf = chunk(s, 0, ms[0])
        hf_ref[pl.ds(r0, L), :] += hk
        r0, hk, mb = chunk(jnp.where(s < nc_ctx, nc_ctx - 1 - s, nc + nc_ctx - 1 - s), 1, ms[1])
        hf_ref[pl.ds(r0, L), :] += hk
        return (mf, mb)

    lax.fori_loop(0, nc, scan_body, (m0, m0))

    def finish_body(s, carry):
        r0 = pl.multiple_of(s * L, L)
        y = hf_ref[pl.ds(r0, L), :]
        y = y * lax.rsqrt(jnp.mean(y * y, axis=-1, keepdims=True) + NORM_EPS) * ng_ref[...]
        y = y * jax.nn.sigmoid(o_ref[0, pl.ds(r0, L), :].astype(f32))
        out_ref[0, pl.ds(r0, L), :] = y.astype(out_ref.dtype)
        return carry

    lax.fori_loop(0, nc, finish_body, 0)


def mlstm_mixer(qkv, o_src, o_off, gate_small, gate_b, norm_g, ctx):
    b, s, _ = qkv.shape
    nh = gate_b.shape[-1]
    vtot = norm_g.shape[-1]
    dv = vtot // nh
    dk = (qkv.shape[-1] - vtot) // (2 * nh)
    L = MLSTM_CHUNK
    nc, nc_ctx = s // L, ctx // L
    assert s % L == 0 and ctx % L == 0 and dk % LANE == 0 and dv % LANE == 0 and o_off % dv == 0
    assert 4 * nh <= LANE and gate_small.shape[-1] == LANE
    gt = chunk_transpose(gate_small, L)
    gb = jnp.pad(gate_b.astype(f32).reshape(4, nh).T, ((0, 0), (0, 4)))
    gb = jnp.broadcast_to(gb[:, :, None], (nh, 8, L))
    gb_c = jnp.pad(gate_b.astype(f32).reshape(1, 4 * nh), ((0, 0), (0, LANE - 4 * nh)))
    ng = norm_g.astype(f32)[None, :]
    koff = nh
    voff = 2 * nh * dk // dv
    ooff = o_off // dv
    return pl.pallas_call(
        functools.partial(_mlstm_kernel, nc_ctx=nc_ctx, nc=nc, dk=dk, dv=dv, n_heads=nh),
        grid=(b, nh),
        in_specs=[
            pl.BlockSpec((1, s, dk), lambda bi, hi: (bi, 0, hi)),
            pl.BlockSpec((1, s, dk), lambda bi, hi: (bi, 0, koff + hi)),
            pl.BlockSpec((1, s, dv), lambda bi, hi: (bi, 0, voff + hi)),
            pl.BlockSpec((1, s, dv), lambda bi, hi: (bi, 0, ooff + hi)),
            pl.BlockSpec((1, nc, LANE, L), lambda bi, hi: (bi, 0, 0, 0)),
            pl.BlockSpec((1, 8, L), lambda bi, hi: (hi, 0, 0)),
            pl.BlockSpec((1, s, LANE), lambda bi, hi: (bi, 0, 0)),
            pl.BlockSpec((1, LANE), lambda bi, hi: (0, 0)),
            pl.BlockSpec((1, dv), lambda bi, hi: (0, hi)),
        ],
        out_specs=pl.BlockSpec((1, s, dv), lambda bi, hi: (bi, 0, hi)),
        out_shape=jax.ShapeDtypeStruct((b, s, vtot), bf16),
        scratch_shapes=[pltpu.VMEM((s, dv), f32), pltpu.VMEM((2, dk, dv + LANE), f32)],
        compiler_params=_cparams(("parallel", "parallel")),
        name="mlstm_mixer",
    )(qkv, qkv, qkv, o_src, gt, gb, gate_small, gb_c, ng)


def _outproj_kernel(y_ref, w_ref, xl_ref, xc_ref, g_ref, ol_ref, oc_ref, *, n_ctx, cols, d):
    j = pl.program_id(1)
    acc = jnp.dot(y_ref[0], w_ref[...], preferred_element_type=f32)

    @pl.when(j < n_ctx)
    def _():
        oc_ref[0] = xc_ref[0] + g_ref[0, 0:1, :] * acc

    @pl.when(j >= n_ctx)
    def _():
        gl = g_ref[0, 1:2, :]
        if cols == 0:
            ol_ref[0] = xl_ref[0] + gl * acc
        else:
            nr = xl_ref.shape[1]
            for ci in range(cols):
                ol_ref[0, :, ci * d:(ci + 1) * d] = xl_ref[0, :, ci * d:(ci + 1) * d] + gl * acc[ci * nr:(ci + 1) * nr, :]


def outproj_residual(y, w, x_lat, x_ctx, gates, col_major):
    b, t, d = x_lat.shape
    ctx = x_ctx.shape[1]
    kin = y.shape[-1]
    tt = ROW_TILE
    n_ctx, n_lat = ctx // tt, t // tt
    if col_major:
        r = t // GRID_W
        cols = tt // r
        xl = x_lat.reshape(b, r, GRID_W * d)
        xl_spec = pl.BlockSpec((1, r, cols * d), lambda bi, j: (bi, 0, jnp.maximum(j - n_ctx, 0)))
    else:
        cols = 0
        xl = x_lat
        xl_spec = pl.BlockSpec((1, tt, d), lambda bi, j: (bi, jnp.maximum(j - n_ctx, 0), 0))
    ol, oc = pl.pallas_call(
        functools.partial(_outproj_kernel, n_ctx=n_ctx, cols=cols, d=d),
        grid=(b, n_ctx + n_lat),
        in_specs=[
            pl.BlockSpec((1, tt, kin), lambda bi, j: (bi, j, 0)),
            pl.BlockSpec((kin, d), lambda bi, j: (0, 0), pipeline_mode=pl.Buffered(1)),
            xl_spec,
            pl.BlockSpec((1, tt, d), lambda bi, j: (bi, jnp.minimum(j, n_ctx - 1), 0)),
            pl.BlockSpec((1, 2, d), lambda bi, j: (bi, 0, 0)),
        ],
        out_specs=[xl_spec, pl.BlockSpec((1, tt, d), lambda bi, j: (bi, jnp.minimum(j, n_ctx - 1), 0))],
        out_shape=[jax.ShapeDtypeStruct(xl.shape, f32), jax.ShapeDtypeStruct(x_ctx.shape, f32)],
        compiler_params=_cparams(("parallel", "arbitrary")),
        name="outproj_residual",
    )(y, w, xl, x_ctx, gates)
    return ol.reshape(b, t, d), oc


def _mod_kernel(c_ref, w_ref, b_ref, o_ref):
    cvec = c_ref[...]
    cs = (cvec * jax.nn.sigmoid(cvec)).astype(bf16)
    o_ref[0] = jnp.dot(cs, w_ref[0].astype(bf16), preferred_element_type=f32) + b_ref[0]


def modulation(c_rows, mod_w, mod_b):
    depth, d, n = mod_w.shape
    tn = 1024
    assert n % tn == 0
    return pl.pallas_call(
        _mod_kernel,
        grid=(depth, n // tn),
        in_specs=[
            pl.BlockSpec(c_rows.shape, lambda i, j: (0, 0)),
            pl.BlockSpec((1, d, tn), lambda i, j: (i, 0, j)),
            pl.BlockSpec((1, 1, tn), lambda i, j: (i, 0, j)),
        ],
        out_specs=pl.BlockSpec((1, c_rows.shape[0], tn), lambda i, j: (i, 0, j)),
        out_shape=jax.ShapeDtypeStruct((depth, c_rows.shape[0], n), f32),
        compiler_params=_cparams(("parallel", "parallel")),
        name="modulation",
    )(c_rows, mod_w, mod_b.reshape(depth, 1, n))


def _moe_kernel(be_ref, nu_ref, x_ref, wgu_ref, bgu_ref, wd_ref, bd_ref, o_ref, *, f):
    i = pl.program_id(0)
    dh = x_ref.shape[1]

    @pl.when(i < nu_ref[0])
    def _():
        xa, xb = _unpack_pair(x_ref[...])
        h = (jnp.dot(xa.astype(bf16), wgu_ref[0, 0, :dh, :], preferred_element_type=f32)
             + jnp.dot(xb.astype(bf16), wgu_ref[0, 0, dh:, :], preferred_element_type=f32) + bgu_ref[0, 0])
        glu = jnp.minimum(h[:, :f], SWIGLU_LIMIT)
        lin = jnp.clip(h[:, f:], -SWIGLU_LIMIT, SWIGLU_LIMIT)
        act = glu * jax.nn.sigmoid(SWIGLU_ALPHA * glu) * (lin + 1.0)
        y = jnp.dot(act.astype(bf16), wd_ref[0, 0], preferred_element_type=f32) + bd_ref[0, 0]
        yb = y.astype(bf16).astype(f32)
        o_ref[...] = _pack_pair(yb[:, :dh], yb[:, dh:])

    @pl.when(i >= nu_ref[0])
    def _():
        o_ref[...] = jnp.zeros_like(o_ref)


def moe_experts(xs, block_e, n_used, layer, gu_w, gu_b, down_w, down_b, tm):
    n_slots, dh = xs.shape
    depth, e, d, f2 = gu_w.shape
    f = f2 // 2
    n_blocks = n_slots // tm
    last = lambda i, nu: jnp.minimum(i, nu[0] - 1)
    grid_spec = pltpu.PrefetchScalarGridSpec(
        num_scalar_prefetch=2,
        grid=(n_blocks,),
        in_specs=[
            pl.BlockSpec((tm, dh), lambda i, be, nu: (last(i, nu), 0)),
            pl.BlockSpec((1, 1, d, f2), lambda i, be, nu: (layer, be[last(i, nu)], 0, 0)),
            pl.BlockSpec((1, 1, 1, f2), lambda i, be, nu: (layer, be[last(i, nu)], 0, 0)),
            pl.BlockSpec((1, 1, f, d), lambda i, be, nu: (layer, be[last(i, nu)], 0, 0)),
            pl.BlockSpec((1, 1, 1, d), lambda i, be, nu: (layer, be[last(i, nu)], 0, 0)),
        ],
        out_specs=pl.BlockSpec((tm, dh), lambda i, be, nu: (i, 0)),
    )
    return pl.pallas_call(
        functools.partial(_moe_kernel, f=f),
        grid_spec=grid_spec,
        out_shape=jax.ShapeDtypeStruct((n_slots, dh), jnp.uint32),
        compiler_params=_cparams(("arbitrary",)),
        name="moe_experts",
    )(block_e, n_used, xs, gu_w, gu_b.reshape(depth, e, 1, f2), down_w, down_b.reshape(depth, e, 1, d))


ROUTE_TILE = 512


def _route_kernel(lg_ref, rb_ref, ri_ref, rg_ref, cnt_ref, run_ref, *, n_exp):
    i = pl.program_id(0)
    tb = lg_ref.shape[0]
    i32 = jnp.int32

    @pl.when(i == 0)
    def _():
        run_ref[...] = jnp.zeros_like(run_ref)

    lane = lax.broadcasted_iota(i32, (tb, LANE), 1)
    lg = jnp.where(lane < n_exp, lg_ref[...] + rb_ref[...], -jnp.inf)
    vals, idxs, hots = [], [], []
    for _ in range(TOP_K):
        m = jnp.max(lg, axis=1, keepdims=True)
        idx = jnp.min(jnp.where(lg == m, lane, LANE), axis=1, keepdims=True)
        hot = lane == idx
        lg = jnp.where(hot, -jnp.inf, lg)
        vals.append(m)
        idxs.append(idx)
        hots.append(hot)
    es = [jnp.exp(v - vals[0]) for v in vals]
    tot = es[0]
    for ev in es[1:]:
        tot = tot + ev
    msum = hots[0].astype(f32)
    for hot in hots[1:]:
        msum = msum + hot.astype(f32)
    tri = (lax.broadcasted_iota(i32, (tb, tb), 0) > lax.broadcasted_iota(i32, (tb, tb), 1)).astype(bf16)
    before = jnp.dot(tri, msum.astype(bf16), preferred_element_type=f32) + run_ref[...]
    ri = jnp.zeros((tb, LANE), i32)
    rg = jnp.zeros((tb, LANE), f32)
    for k in range(TOP_K):
        rank = jnp.sum(jnp.where(hots[k], before, 0.0), axis=1, keepdims=True)
        ri = jnp.where(lane == k, idxs[k], ri)
        ri = jnp.where(lane == TOP_K + k, rank.astype(i32), ri)
        rg = jnp.where(lane == k, es[k] / tot, rg)
    ri_ref[...] = ri
    rg_ref[...] = rg
    run_ref[...] = run_ref[...] + jnp.sum(msum, axis=0, keepdims=True)
    cnt_ref[...] = run_ref[...]


def route(logits, router_b, n_exp):
    n = logits.shape[0]
    tb = _pick_tile(n, (ROUTE_TILE, 256))
    return pl.pallas_call(
        functools.partial(_route_kernel, n_exp=n_exp),
        grid=(n // tb,),
        in_specs=[pl.BlockSpec((tb, LANE), lambda i: (i, 0)), pl.BlockSpec((1, LANE), lambda i: (0, 0))],
        out_specs=[pl.BlockSpec((tb, LANE), lambda i: (i, 0)), pl.BlockSpec((tb, LANE), lambda i: (i, 0)),
                   pl.BlockSpec((1, LANE), lambda i: (0, 0))],
        out_shape=[jax.ShapeDtypeStruct((n, LANE), jnp.int32), jax.ShapeDtypeStruct((n, LANE), f32),
                   jax.ShapeDtypeStruct((1, LANE), f32)],
        scratch_shapes=[pltpu.VMEM((1, LANE), f32)],
        compiler_params=_cparams(("arbitrary",)),
        name="route",
    )(logits, router_b)


def _row_copy(src, s_row, dst, d_row, sem):
    return pltpu.make_async_copy(src.at[pl.ds(s_row, 1)], dst.at[pl.ds(d_row, 1)], sem)


def _dispatch_kernel(zs_ref, zc_ref, nu_ref, slot_ref, u_ref, xs_ref, zero_ref, sem, *, n_exp, tm):
    i = pl.program_id(0)
    tb = u_ref.shape[0]
    n_blocks = xs_ref.shape[0] // tm

    @pl.when(i == 0)
    def _():
        zero_ref[...] = jnp.zeros_like(zero_ref)

        def block_copy(blk):
            return pltpu.make_async_copy(zero_ref, xs_ref.at[pl.ds(pl.multiple_of(blk * tm, tm), tm)], sem)

        def start_block(blk, carry):
            block_copy(blk).start()
            return carry

        def wait_block(blk, carry):
            block_copy(blk).wait()
            return carry

        lax.fori_loop(nu_ref[0], n_blocks, start_block, 0)
        lax.fori_loop(nu_ref[0], n_blocks, wait_block, 0)

        def per_expert(e, carry):
            def start(r, c):
                _row_copy(zero_ref, 0, xs_ref, zs_ref[e] + r, sem).start()
                return c

            def wait(r, c):
                _row_copy(zero_ref, 0, xs_ref, 0, sem).wait()
                return c

            lax.fori_loop(0, zc_ref[e], start, 0)
            lax.fori_loop(0, zc_ref[e], wait, 0)
            return carry

        lax.fori_loop(0, n_exp, per_expert, 0)

    def start(r, c):
        for k in range(TOP_K):
            _row_copy(u_ref, r, xs_ref, slot_ref[0, 0, r * TOP_K + k], sem).start(priority=k % 2)
        return c

    def wait(r, c):
        _row_copy(u_ref, 0, xs_ref, 0, sem).wait()
        return c

    lax.fori_loop(0, tb, start, 0, unroll=4)
    lax.fori_loop(0, tb * TOP_K, wait, 0, unroll=8)


def dispatch(u, slot, zero_start, zero_count, n_used, n_blocks, tm):
    n, w = u.shape
    tb = ROW_TILE
    n_exp = zero_start.shape[0]
    grid_spec = pltpu.PrefetchScalarGridSpec(
        num_scalar_prefetch=3,
        grid=(n // tb,),
        in_specs=[
            pl.BlockSpec((1, 1, tb * TOP_K), lambda i, zs, zc, nu: (i, 0, 0), memory_space=pltpu.SMEM),
            pl.BlockSpec((tb, w), lambda i, zs, zc, nu: (i, 0)),
        ],
        out_specs=pl.BlockSpec(memory_space=pl.ANY),
        scratch_shapes=[pltpu.VMEM((tm, w), jnp.uint32), pltpu.SemaphoreType.DMA],
    )
    return pl.pallas_call(
        functools.partial(_dispatch_kernel, n_exp=n_exp, tm=tm),
        grid_spec=grid_spec,
        out_shape=jax.ShapeDtypeStruct((n_blocks * tm, w), jnp.uint32),
        compiler_params=_cparams(("arbitrary",)),
        name="dispatch",
    )(zero_start, zero_count, n_used, slot.reshape(n // tb, 1, tb * TOP_K), u)


def _combine_kernel(slot_ref, rg_ref, xl_ref, xc_ref, g_ref, ys_ref, ol_ref, oc_ref, buf_ref, sem, *, n_ctx):
    j = pl.program_id(1)
    tb = rg_ref.shape[0]

    def start(r, c):
        for k in range(TOP_K):
            pltpu.make_async_copy(ys_ref.at[pl.ds(slot_ref[0, 0, r * TOP_K + k], 1)],
                                  buf_ref.at[k, pl.ds(r, 1)], sem).start(priority=k % 2)
        return c

    def wait(r, c):
        pltpu.make_async_copy(ys_ref.at[pl.ds(0, 1)], buf_ref.at[0, pl.ds(0, 1)], sem).wait()
        return c

    lax.fori_loop(0, tb, start, 0, unroll=4)
    lax.fori_loop(0, tb * TOP_K, wait, 0, unroll=8)

    gates = rg_ref[...]
    acc = None
    for k in range(TOP_K):
        ya, yb = _unpack_pair(buf_ref[k])
        term = gates[:, k:k + 1] * jnp.concatenate([ya, yb], axis=1)
        acc = term if acc is None else acc + term

    @pl.when(j < n_ctx)
    def _():
        oc_ref[0] = xc_ref[0] + g_ref[0, 0:1, :] * acc

    @pl.when(j >= n_ctx)
    def _():
        ol_ref[0] = xl_ref[0] + g_ref[0, 1:2, :] * acc


def combine(ys, slot, rg, x_lat, x_ctx, gates):
    b, t, d = x_lat.shape
    ctx = x_ctx.shape[1]
    tb = ROW_TILE
    n_ctx, n_lat = ctx // tb, t // tb
    steps = n_ctx + n_lat
    return pl.pallas_call(
        functools.partial(_combine_kernel, n_ctx=n_ctx),
        grid=(b, steps),
        in_specs=[
            pl.BlockSpec((1, 1, tb * TOP_K), lambda bi, j: (bi * steps + j, 0, 0), memory_space=pltpu.SMEM),
            pl.BlockSpec((tb, LANE), lambda bi, j: (bi * steps + j, 0)),
            pl.BlockSpec((1, tb, d), lambda bi, j: (bi, jnp.maximum(j - n_ctx, 0), 0)),
            pl.BlockSpec((1, tb, d), lambda bi, j: (bi, jnp.minimum(j, n_ctx - 1), 0)),
            pl.BlockSpec((1, 2, d), lambda bi, j: (bi, 0, 0)),
            pl.BlockSpec(memory_space=pl.ANY),
        ],
        out_specs=[pl.BlockSpec((1, tb, d), lambda bi, j: (bi, jnp.maximum(j - n_ctx, 0), 0)),
                   pl.BlockSpec((1, tb, d), lambda bi, j: (bi, jnp.minimum(j, n_ctx - 1), 0))],
        out_shape=[jax.ShapeDtypeStruct(x_lat.shape, f32), jax.ShapeDtypeStruct(x_ctx.shape, f32)],
        scratch_shapes=[pltpu.VMEM((TOP_K, tb, d // 2), jnp.uint32), pltpu.SemaphoreType.DMA],
        compiler_params=_cparams(("arbitrary", "arbitrary")),
        name="combine",
    )(slot.reshape(b * steps, 1, tb * TOP_K), rg, x_lat, x_ctx, gates, ys)


def _final_norm_kernel(x_ref, g_ref, o_ref):
    x = x_ref[0]
    o_ref[0] = (x * lax.rsqrt(jnp.mean(x * x, axis=-1, keepdims=True) + NORM_EPS)) * g_ref[...]


def final_norm(x, g):
    b, t, d = x.shape
    tt = 512 if t % 512 == 0 else ROW_TILE
    return pl.pallas_call(
        _final_norm_kernel,
        grid=(b, t // tt),
        in_specs=[pl.BlockSpec((1, tt, d), lambda bi, j: (bi, j, 0)), pl.BlockSpec((1, d), lambda bi, j: (0, 0))],
        out_specs=pl.BlockSpec((1, tt, d), lambda bi, j: (bi, j, 0)),
        out_shape=jax.ShapeDtypeStruct((b, t, d), f32),
        compiler_params=_cparams(("parallel", "parallel")),
        name="final_norm",
    )(x, g)


MOE_TILE = 512


def _pick_tile(n, prefs):
    for p in prefs:
        if n % p == 0:
            return p
    raise ValueError(f"no tile in {prefs} divides {n}")


def _pad_lanes(w):
    return jnp.pad(w, ((0, 0), (0, LANE - w.shape[1])))


def _project(h, w_main):
    b, s, d = h.shape
    m = b * s
    n = w_main.shape[1]
    out = matmul(h.reshape(m, d), w_main, bf16, _pick_tile(m, (1024, 512, 256)), _pick_tile(n, (1024, 512, 256, 128)))
    return out.reshape(b, s, n)


def _moe_ffn(u, logits, router_b, layer, gu_w, gu_b, down_w, down_b, x_lat, x_ctx, gates):
    n = u.shape[0]
    e = gu_w.shape[1]
    tm = MOE_TILE
    i32 = jnp.int32
    ri, rg, counts = route(logits, _pad_lanes(router_b.astype(f32)[None, :]), e)
    cnt = counts[0, :e].astype(i32)
    padded = (cnt + tm - 1) // tm * tm
    pad_end = jnp.cumsum(padded)
    pad_start = pad_end - padded
    n_blocks = -(-(n * TOP_K) // tm) + e
    top_idx, rank = ri[:, :TOP_K], ri[:, TOP_K:2 * TOP_K]
    hot = top_idx[:, :, None] == jnp.arange(e, dtype=i32)[None, None, :]
    slot = (jnp.sum(jnp.where(hot, pad_start[None, None, :], 0), axis=-1) + rank).reshape(-1)
    block_start = jnp.arange(n_blocks, dtype=i32) * tm
    block_e = jnp.minimum(jnp.sum((block_start[:, None] >= pad_end[None, :]).astype(i32), axis=1), e - 1)
    n_used = (pad_end[-1] // tm).astype(i32).reshape(1)
    xs = dispatch(u, slot, pad_start + cnt, padded - cnt, n_used, n_blocks, tm)
    ys = moe_experts(xs, block_e, n_used, layer, gu_w, gu_b, down_w, down_b, tm)
    return combine(ys, slot, rg, x_lat, x_ctx, gates)


def kernel(x, c, ctx, c_ctx, mod_w, mod_b, norm1_g, norm2_g, final_g, m_in_w, m_conv_w, m_conv_b, m_dt_bias, m_a_log, m_d, m_norm_g, m_out_w, l_in_w, l_conv_w, l_conv_b, l_gate_b, l_norm_g, l_out_w, router_w, router_b, e_gu_w, e_gu_b, e_down_w, e_down_b):
    b, t, d = x.shape
    n_ctx_tok = ctx.shape[1]
    depth = mod_w.shape[0]
    m_inner = m_out_w.shape[1]
    m_conv = m_conv_w.shape[-1]
    l_qkv = l_conv_w.shape[-1]
    l_vtot = l_out_w.shape[1]
    gu_w, down_w = e_gu_w.astype(bf16), e_down_w.astype(bf16)

    c_rows = jnp.zeros((16, d), f32).at[:b].set(c).at[b].set(c_ctx)
    mod_all = modulation(c_rows, mod_w, mod_b)

    xl, xc = x, ctx
    for i in range(depth):
        j = i // 2
        mod = mod_all[i].reshape(16, 6, d)
        lat, cx = mod[:b], jnp.broadcast_to(mod[b][None], (b, 6, d))

        def mods(k):
            return jnp.stack([cx[:, k], cx[:, k + 1], lat[:, k], lat[:, k + 1]], axis=1)

        def gate(k):
            return jnp.stack([cx[:, k], lat[:, k]], axis=1)

        if i % 2 == 0:
            w_in = m_in_w[j].astype(bf16)
            n_main = m_inner + m_conv
            h, dt_small = normmod(xl, xc, norm1_g[i][None], mods(0), _pad_lanes(w_in[:, n_main:]), False)
            proj = _project(h, w_in[:, :n_main])
            xbc = conv_silu(proj, m_conv_w[j], m_conv_b[j][None], m_inner, 512, n_ctx_tok)
            y = ssd_mixer(xbc, proj, dt_small, m_dt_bias[j], m_a_log[j], m_d[j], m_norm_g[j], n_ctx_tok)
            xl, xc = outproj_residual(y, m_out_w[j].astype(bf16), xl, xc, gate(2), False)
        else:
            w_in = l_in_w[j].astype(bf16)
            n_main = l_qkv + l_vtot
            h, gate_small = normmod(xl, xc, norm1_g[i][None], mods(0), _pad_lanes(w_in[:, n_main:]), True)
            proj = _project(h, w_in[:, :n_main])
            qkv = conv_silu(proj, l_conv_w[j], l_conv_b[j][None], 0, 512, n_ctx_tok)
            y = mlstm_mixer(qkv, proj, l_qkv, gate_small, l_gate_b[j], l_norm_g[j], n_ctx_tok)
            xl, xc = outproj_residual(y, l_out_w[j].astype(bf16), xl, xc, gate(2), True)

        u, logits = normmod(xl, xc, norm2_g[i][None], mods(3), _pad_lanes(router_w[i].astype(bf16)), False, pack=True)
        s = n_ctx_tok + t
        xl, xc = _moe_ffn(u.reshape(b * s, d // 2), logits.reshape(b * s, LANE), router_b[i], i, gu_w, e_gu_b,
                          down_w, e_down_b, xl, xc, gate(5))
    return final_norm(xl, final_g[None])
```

```python
import functools

import jax
import jax.numpy as jnp
from jax import lax
from jax.experimental import pallas as pl
from jax.experimental.pallas import tpu as pltpu

f32 = jnp.float32
bf16 = jnp.bfloat16

NORM_EPS = 1e-6
CONV_W = 5
GRID_W = 64
M_D_STATE = 128
TOP_K = 4
SWIGLU_LIMIT = 7.0
SWIGLU_ALPHA = 1.702
SSD_CHUNK = 128
MLSTM_CHUNK = 256
ROW_TILE = 256
LANE = 128
V7X_VMEM_LIMIT = 60000 * 1024


def _cparams(sem):
    return pltpu.CompilerParams(dimension_semantics=sem, vmem_limit_bytes=V7X_VMEM_LIMIT)


def _pack_pair(a, b):
    u32 = jnp.uint32
    return (lax.bitcast_convert_type(a, u32) & u32(0xFFFF0000)) | (lax.bitcast_convert_type(b, u32) >> 16)


def _unpack_pair(w):
    u32 = jnp.uint32
    return (lax.bitcast_convert_type(w & u32(0xFFFF0000), f32), lax.bitcast_convert_type(w << 16, f32))


def _col_view(x_lat, tt, n_ctx):
    b, t, d = x_lat.shape
    r = t // GRID_W
    cols = tt // r
    assert tt % r == 0 and GRID_W % cols == 0
    spec = pl.BlockSpec((1, r, cols * d), lambda bi, j: (bi, 0, jnp.maximum(j - n_ctx, 0)))
    return x_lat.reshape(b, r, GRID_W * d), spec, cols


def _normmod_kernel(xl_ref, xc_ref, g_ref, mod_ref, ws_ref, h_ref, small_ref, *, n_ctx, cols, d, pack):
    j = pl.program_id(1)

    def rows(x, shift, scale, r0, nr):
        ms = jnp.mean(x * x, axis=-1, keepdims=True)
        y = (x * lax.rsqrt(ms + NORM_EPS)) * g_ref[...]
        hb = (y * (1.0 + scale) + shift).astype(bf16)
        if pack:
            hf = hb.astype(f32)
            h_ref[0, r0:r0 + nr, :] = _pack_pair(hf[:, :d // 2], hf[:, d // 2:])
        else:
            h_ref[0, r0:r0 + nr, :] = hb
        small_ref[0, r0:r0 + nr, :] = jnp.dot(hb, ws_ref[...], preferred_element_type=f32)

    @pl.when(j < n_ctx)
    def _():
        rows(xc_ref[0], mod_ref[0, 0:1, :], mod_ref[0, 1:2, :], 0, xc_ref.shape[1])

    @pl.when(j >= n_ctx)
    def _():
        if cols == 0:
            rows(xl_ref[0], mod_ref[0, 2:3, :], mod_ref[0, 3:4, :], 0, xl_ref.shape[1])
        else:
            nr = xl_ref.shape[1]
            for ci in range(cols):
                rows(xl_ref[0, :, ci * d:(ci + 1) * d], mod_ref[0, 2:3, :], mod_ref[0, 3:4, :], ci * nr, nr)


def normmod(x_lat, x_ctx, g, mods, w_small, col_major, pack=False):
    b, t, d = x_lat.shape
    ctx = x_ctx.shape[1]
    tt = ROW_TILE
    n_ctx, n_lat = ctx // tt, t // tt
    assert ctx % tt == 0 and t % tt == 0
    if col_major:
        xl, xl_spec, cols = _col_view(x_lat, tt, n_ctx)
    else:
        cols = 0
        xl = x_lat
        xl_spec = pl.BlockSpec((1, tt, d), lambda bi, j: (bi, jnp.maximum(j - n_ctx, 0), 0))
    s = ctx + t
    dh = d // 2 if pack else d
    return pl.pallas_call(
        functools.partial(_normmod_kernel, n_ctx=n_ctx, cols=cols, d=d, pack=pack),
        grid=(b, n_ctx + n_lat),
        in_specs=[
            xl_spec,
            pl.BlockSpec((1, tt, d), lambda bi, j: (bi, jnp.minimum(j, n_ctx - 1), 0)),
            pl.BlockSpec((1, d), lambda bi, j: (0, 0)),
            pl.BlockSpec((1, 4, d), lambda bi, j: (bi, 0, 0)),
            pl.BlockSpec((d, LANE), lambda bi, j: (0, 0)),
        ],
        out_specs=[
            pl.BlockSpec((1, tt, dh), lambda bi, j: (bi, j, 0)),
            pl.BlockSpec((1, tt, LANE), lambda bi, j: (bi, j, 0)),
        ],
        out_shape=[jax.ShapeDtypeStruct((b, s, dh), jnp.uint32 if pack else bf16),
                   jax.ShapeDtypeStruct((b, s, LANE), f32)],
        compiler_params=_cparams(("parallel", "arbitrary")),
        name="normmod",
    )(xl, x_ctx, g, mods, w_small)


def _mm_kernel(x_ref, w_ref, o_ref):
    o_ref[...] = jnp.dot(x_ref[...], w_ref[...], preferred_element_type=f32).astype(o_ref.dtype)


def matmul(x, w, out_dtype, tm, tn):
    m, k = x.shape
    n = w.shape[1]
    assert m % tm == 0 and n % tn == 0
    return pl.pallas_call(
        _mm_kernel,
        grid=(m // tm, n // tn),
        in_specs=[pl.BlockSpec((tm, k), lambda i, j: (i, 0)), pl.BlockSpec((k, tn), lambda i, j: (0, j))],
        out_specs=pl.BlockSpec((tm, tn), lambda i, j: (i, j)),
        out_shape=jax.ShapeDtypeStruct((m, n), out_dtype),
        compiler_params=_cparams(("parallel", "arbitrary")),
        name="matmul",
    )(x, w)


def _conv_kernel(u_ref, w_ref, b_ref, o_ref, pad_ref, *, ctx, tile):
    s, cb = u_ref.shape[1], u_ref.shape[2]
    half = CONV_W // 2
    zero8 = jnp.zeros((8, cb), f32)
    pad_ref[0:8, :] = zero8
    pad_ref[8 + ctx:16 + ctx, :] = zero8
    pad_ref[16 + s:24 + s, :] = zero8
    for r0 in range(0, s, tile):
        base = 8 if r0 < ctx else 16
        pad_ref[base + r0:base + r0 + tile, :] = u_ref[0, r0:r0 + tile, :].astype(f32)
    w = w_ref[...]
    bias = b_ref[...]
    for r0 in range(0, s, tile):
        base = 8 if r0 < ctx else 16
        big = pad_ref[base + r0 - 8:base + r0 + tile + 8, :]
        acc = bias + w[half:half + 1, :] * big[8:8 + tile, :]
        for k in range(CONV_W):
            if k != half:
                rolled = pltpu.roll(big, (half - k) % (tile + 16), axis=0)
                acc = acc + w[k:k + 1, :] * rolled[8:8 + tile, :]
        o_ref[0, r0:r0 + tile, :] = (acc * jax.nn.sigmoid(acc)).astype(o_ref.dtype)


def conv_silu(u, w, bias, lane_off, cb, ctx):
    b, s, _ = u.shape
    c = w.shape[1]
    assert c % cb == 0 and lane_off % cb == 0
    tile = ROW_TILE
    assert ctx % tile == 0 and s % tile == 0
    off = lane_off // cb
    return pl.pallas_call(
        functools.partial(_conv_kernel, ctx=ctx, tile=tile),
        grid=(b, c // cb),
        in_specs=[
            pl.BlockSpec((1, s, cb), lambda bi, j: (bi, 0, off + j)),
            pl.BlockSpec((CONV_W, cb), lambda bi, j: (0, j)),
            pl.BlockSpec((1, cb), lambda bi, j: (0, j)),
        ],
        out_specs=pl.BlockSpec((1, s, cb), lambda bi, j: (bi, 0, j)),
        out_shape=jax.ShapeDtypeStruct((b, s, c), bf16),
        scratch_shapes=[pltpu.VMEM((s + 24, cb), f32)],
        compiler_params=_cparams(("parallel", "parallel")),
        name="conv_silu",
    )(u, w, bias)


def _split3(a):
    hi = a.astype(bf16)
    r1 = a - hi.astype(f32)
    mid = r1.astype(bf16)
    lo = (r1 - mid.astype(f32)).astype(bf16)
    return jnp.concatenate([hi, mid, lo], axis=0)


def _softplus(x):
    return jnp.maximum(x, 0.0) + jnp.log1p(jnp.exp(-jnp.abs(x)))


def _dot_t(a, b):
    return lax.dot_general(a, b, (((0,), (0,)), ((), ())), preferred_element_type=f32)


def _dot_nt(a, b):
    return lax.dot_general(a, b, (((1,), (1,)), ((), ())), preferred_element_type=f32)


def _chunk_transpose_kernel(x_ref, o_ref):
    nc, _, chunk = o_ref.shape[1:]
    for ci in range(nc):
        o_ref[0, ci] = x_ref[0, ci * chunk:(ci + 1) * chunk, :].T


def chunk_transpose(x, chunk):
    b, s, w = x.shape
    nc = s // chunk
    return pl.pallas_call(
        _chunk_transpose_kernel,
        grid=(b,),
        in_specs=[pl.BlockSpec((1, s, w), lambda bi: (bi, 0, 0))],
        out_specs=pl.BlockSpec((1, nc, w, chunk), lambda bi: (bi, 0, 0, 0)),
        out_shape=jax.ShapeDtypeStruct((b, nc, w, chunk), f32),
        compiler_params=_cparams(("parallel",)),
        name="chunk_transpose",
    )(x)


def _hi_mid(v):
    hi = v.astype(bf16)
    return jnp.concatenate([hi, (v - hi.astype(f32)).astype(bf16)], axis=1)


def _ssd_kernel(x_ref, b_ref, c_ref, z_ref, dt_ref, bias_ref, a_ref, dtc_ref, biasc_ref, ac_ref, dskip_ref, ng_ref,
                o_ref, yf_ref, h_ref, earg_ref, echan_ref, *, nc_ctx, nc, heads, headdim, n_heads):
    L = SSD_CHUNK
    R, P = heads, headdim
    RP = R * P
    i32 = jnp.int32
    gi = pl.program_id(1)

    row = lax.broadcasted_iota(i32, (L, L), 0)
    col = lax.broadcasted_iota(i32, (L, L), 1)
    lane_lo = lax.broadcasted_iota(i32, (L, 2 * P), 1) < P
    tri_c = jnp.tile((col <= row).astype(bf16), (1, 2))

    for d in range(2):
        first = d * n_heads + gi * R
        er = lax.broadcasted_iota(i32, (2 * LANE, R * L), 0) % LANE
        ec = lax.broadcasted_iota(i32, (2 * LANE, R * L), 1) // L
        earg_ref[d] = (er == first + ec).astype(bf16)
        fr = lax.broadcasted_iota(i32, (2 * LANE, RP), 0) % LANE
        fc = lax.broadcasted_iota(i32, (2 * LANE, RP), 1) // P
        echan_ref[d] = (fr == first + fc).astype(bf16)

    def chunk(ci, d):
        fwd = d == 0
        tri = ((row <= col) if fwd else (row >= col)).astype(bf16)
        mask = (col <= row) if fwd else (col >= row)
        r0 = pl.multiple_of(ci * L, L)
        raw = dt_ref[0, ci, pl.ds(pl.multiple_of(d * n_heads + gi * R, 8), R), :]
        dt = _softplus(raw + bias_ref[0, d * R:(d + 1) * R, :])
        a = dt * a_ref[0, d * R:(d + 1) * R, :]
        cum3 = jnp.dot(_split3(a), tri, preferred_element_type=f32)
        cum = cum3[0:R] + cum3[R:2 * R] + cum3[2 * R:3 * R]
        srow = cum - jnp.log(jnp.maximum(dt, 1e-37))
        crow = jnp.concatenate([srow[h:h + 1, :] for h in range(R)], axis=1)
        dtc = _softplus(dtc_ref[0, pl.ds(r0, L), :] + biasc_ref[...])
        ac = dtc * ac_ref[...]
        ac_hi = ac.astype(bf16)
        ac_parts = jnp.concatenate([ac_hi, (ac - ac_hi.astype(f32)).astype(bf16)], axis=0)
        pre = jnp.dot(tri_c, ac_parts, preferred_element_type=f32)
        tot = pre[L - 1:L, :]
        cumc = pre if fwd else tot - pre + ac
        arg = jnp.dot(_hi_mid(cumc), earg_ref[d], preferred_element_type=f32) - crow
        ecum_x = jnp.dot(_hi_mid(jnp.exp(cumc)), echan_ref[d], preferred_element_type=f32)
        ws_x = jnp.dot(_hi_mid(jnp.exp(tot - cumc) * dtc), echan_ref[d], preferred_element_type=f32)
        etot_x = ecum_x[L - 1:L, :] if fwd else ecum_x[0:1, :]

        xc = x_ref[0, pl.ds(r0, L), :]
        bc = b_ref[0, pl.ds(r0, L), :]
        cc = c_ref[0, pl.ds(r0, L), :]
        cb = _dot_nt(cc, bc)
        ys = []
        for p in range(R // 2):
            w_pair = []
            for h in (2 * p, 2 * p + 1):
                dec = jnp.exp(jnp.where(mask, arg[:, h * L:(h + 1) * L], -jnp.inf))
                w_pair.append((dec * cb).astype(bf16))
            xp = xc[:, 2 * p * P:2 * (p + 1) * P]
            zero = jnp.zeros_like(xp)
            rhs = jnp.concatenate([jnp.where(lane_lo, xp, zero), jnp.where(lane_lo, zero, xp)], axis=0)
            ys.append(jnp.dot(jnp.concatenate(w_pair, axis=1), rhs, preferred_element_type=f32))
        y_intra = jnp.concatenate(ys, axis=1)
        ht = h_ref[d]
        y = y_intra + jnp.dot(cc, ht.astype(bf16), preferred_element_type=f32) * ecum_x
        xw = (xc.astype(f32) * ws_x).astype(bf16)
        h_ref[d] = ht * etot_x + _dot_t(bc, xw)
        return r0, y

    h_ref[...] = jnp.zeros_like(h_ref)
    yf_ref[...] = jnp.zeros_like(yf_ref)

    def scan_body(s, carry):
        r0, y = chunk(s, 0)
        yf_ref[pl.ds(r0, L), :] += y
        r0, y = chunk(jnp.where(s < nc_ctx, nc_ctx - 1 - s, nc + nc_ctx - 1 - s), 1)
        yf_ref[pl.ds(r0, L), :] += y
        return carry

    lax.fori_loop(0, nc, scan_body, 0)

    def finish_body(s, carry):
        r0 = pl.multiple_of(s * L, L)
        y = yf_ref[pl.ds(r0, L), :] + dskip_ref[...] * x_ref[0, pl.ds(r0, L), :].astype(f32)
        z = z_ref[0, pl.ds(r0, L), :].astype(f32)
        y = y * (z * jax.nn.sigmoid(z))
        y = y * lax.rsqrt(jnp.mean(y * y, axis=-1, keepdims=True) + NORM_EPS)
        o_ref[0, pl.ds(r0, L), :] = (y * ng_ref[...]).astype(o_ref.dtype)
        return carry

    lax.fori_loop(0, nc, finish_body, 0)


def ssd_mixer(xbc, z_src, dt_small, dt_bias, a_log, d_skip, norm_g, ctx):
    b, s, _ = xbc.shape
    n = M_D_STATE
    nh = a_log.shape[-1]
    inner = norm_g.shape[-1]
    p = inner // nh
    g = (xbc.shape[-1] - inner) // (2 * n)
    r = nh // g
    rp = r * p
    L = SSD_CHUNK
    nc, nc_ctx = s // L, ctx // L
    assert p * 2 == LANE and s % L == 0 and ctx % L == 0 and r % 8 == 0 and rp % LANE == 0
    dt_t = chunk_transpose(dt_small, L)

    def per_head(v):
        v = v.astype(f32).reshape(2, g, r).transpose(1, 0, 2).reshape(g, 2 * r, 1)
        return jnp.broadcast_to(v, (g, 2 * r, L))

    bias_t = per_head(dt_bias)
    neg_a = -jnp.exp(a_log.astype(f32))
    a_t = per_head(neg_a)
    assert 2 * nh <= LANE and dt_small.shape[-1] == LANE
    bias_c = jnp.pad(dt_bias.astype(f32).reshape(1, 2 * nh), ((0, 0), (0, LANE - 2 * nh)))
    a_c = jnp.pad(neg_a.reshape(1, 2 * nh), ((0, 0), (0, LANE - 2 * nh)))
    dskip = jnp.repeat(d_skip.astype(f32), p)[None, :]
    ng = norm_g.astype(f32)[None, :]
    boff = inner // n
    return pl.pallas_call(
        functools.partial(_ssd_kernel, nc_ctx=nc_ctx, nc=nc, heads=r, headdim=p, n_heads=nh),
        grid=(b, g),
        in_specs=[
            pl.BlockSpec((1, s, rp), lambda bi, gi: (bi, 0, gi)),
            pl.BlockSpec((1, s, n), lambda bi, gi: (bi, 0, boff + gi)),
            pl.BlockSpec((1, s, n), lambda bi, gi: (bi, 0, boff + g + gi)),
            pl.BlockSpec((1, s, rp), lambda bi, gi: (bi, 0, gi)),
            pl.BlockSpec((1, nc, LANE, L), lambda bi, gi: (bi, 0, 0, 0)),
            pl.BlockSpec((1, 2 * r, L), lambda bi, gi: (gi, 0, 0)),
            pl.BlockSpec((1, 2 * r, L), lambda bi, gi: (gi, 0, 0)),
            pl.BlockSpec((1, s, LANE), lambda bi, gi: (bi, 0, 0)),
            pl.BlockSpec((1, LANE), lambda bi, gi: (0, 0)),
            pl.BlockSpec((1, LANE), lambda bi, gi: (0, 0)),
            pl.BlockSpec((1, rp), lambda bi, gi: (0, gi)),
            pl.BlockSpec((1, rp), lambda bi, gi: (0, gi)),
        ],
        out_specs=pl.BlockSpec((1, s, rp), lambda bi, gi: (bi, 0, gi)),
        out_shape=jax.ShapeDtypeStruct((b, s, inner), bf16),
        scratch_shapes=[pltpu.VMEM((s, rp), f32), pltpu.VMEM((2, n, rp), f32),
                        pltpu.VMEM((2, 2 * LANE, r * L), bf16), pltpu.VMEM((2, 2 * LANE, rp), bf16)],
        compiler_params=_cparams(("parallel", "parallel")),
        name="ssd_mixer",
    )(xbc, xbc, xbc, z_src, dt_t, bias_t, a_t, dt_small, bias_c, a_c, dskip, ng)


def _stack_rows(rows, n):
    w = next(r for r in rows if r is not None).shape[1]
    rid = lax.broadcasted_iota(jnp.int32, (n, w), 0)
    out = jnp.zeros((n, w), f32)
    for i, r in enumerate(rows):
        if r is not None:
            out = jnp.where(rid == i, jnp.broadcast_to(r, (n, w)), out)
    return out


def _parts3(a):
    hi = a.astype(bf16).astype(f32)
    mid = (a - hi).astype(bf16).astype(f32)
    lo = a - hi - mid
    return [hi, mid, lo]


def _mlstm_kernel(q_ref, k_ref, v_ref, o_ref, g_ref, gb_ref, gc_ref, gbc_ref, ng_ref, out_ref, hf_ref, c_ref,
                  *, nc_ctx, nc, dk, dv, n_heads):
    L = MLSTM_CHUNK
    i32 = jnp.int32
    scale = dk ** -0.5
    hi_ = pl.program_id(1)
    row = lax.broadcasted_iota(i32, (L, L), 0)
    col = lax.broadcasted_iota(i32, (L, L), 1)
    lane_e = lax.broadcasted_iota(i32, (L, LANE), 1)
    tri_c = jnp.tile((col <= row).astype(bf16), (1, 3))

    def chunk(ci, d, m):
        fwd = d == 0
        tri = ((row <= col) if fwd else (row >= col)).astype(bf16)
        mask = (col <= row) if fwd else (col >= row)
        r0 = pl.multiple_of(ci * L, L)
        lane_i = (2 * d) * n_heads + hi_
        ip = g_ref[0, ci, pl.ds(lane_i, 1), :] + gb_ref[0, 2 * d:2 * d + 1, :]
        logf = -_softplus(-(g_ref[0, ci, pl.ds(lane_i + n_heads, 1), :] + gb_ref[0, 2 * d + 1:2 * d + 2, :]))
        cum3 = jnp.dot(_stack_rows(_parts3(logf), 16).astype(bf16), tri, preferred_element_type=f32)
        bb = cum3[0:1] + cum3[1:2] + cum3[2:3]
        btot = bb[:, L - 1:L] if fwd else bb[:, 0:1]
        u = ip - bb
        gcol = gc_ref[0, pl.ds(r0, L), :] + gbc_ref[...]
        lfc = -_softplus(-gcol)
        pre = jnp.dot(tri_c, _split3(lfc), preferred_element_type=f32)
        cumc = pre if fwd else pre[L - 1:L, :] - pre + lfc
        b_col = jnp.sum(jnp.where(lane_e == lane_i + n_heads, cumc, 0.0), axis=1, keepdims=True)
        u_col = jnp.sum(jnp.where(lane_e == lane_i, gcol, 0.0), axis=1, keepdims=True) - b_col
        dm = jnp.where(mask, b_col + u, -jnp.inf)
        inter = b_col + m
        m_out = jnp.maximum(inter, jnp.max(dm, axis=1, keepdims=True))
        q = q_ref[0, pl.ds(r0, L), :]
        k = k_ref[0, pl.ds(r0, L), :]
        v = v_ref[0, pl.ds(r0, L), :]
        w = jnp.exp(dm - m_out) * (_dot_nt(q, k) * scale)
        inter_w = jnp.exp(inter - m_out)
        cx = c_ref[d]
        qc = jnp.dot(q, cx.astype(bf16), preferred_element_type=f32) * scale
        num = jnp.dot(w.astype(bf16), v, preferred_element_type=f32) + inter_w * qc[:, :dv]
        den = jnp.sum(w, axis=1, keepdims=True) + inter_w * (qc[:, dv:dv + 1] + qc[:, dv + 1:dv + 2])
        hk = num / jnp.maximum(jnp.abs(den), jnp.exp(-m_out))
        m_new = jnp.maximum(btot + m, btot + jnp.max(u, axis=1, keepdims=True))
        e_col = jnp.exp(u_col + btot - m_new)
        e_hi = e_col.astype(bf16).astype(f32)
        e_ext = jnp.where(lane_e == 0, e_hi, jnp.where(lane_e == 1, e_col - e_hi, 0.0))
        vext = jnp.concatenate([v.astype(f32) * e_col, e_ext], axis=1).astype(bf16)
        c_ref[d] = jnp.exp(btot + m - m_new) * cx + _dot_t(k, vext)
        return r0, hk, m_new

    m0 = jnp.full((1, 1), -jnp.inf, f32)
    c_ref[...] = jnp.zeros_like(c_ref)
    hf_ref[...] = jnp.zeros_like(hf_ref)

    def scan_body(s, ms):
        r0, hk, mf = chunk(s, 0, ms[0])
        hf_ref[pl.ds(r0, L), :] += hk
        r0, hk, mb = chunk(jnp.where(s < nc_ctx, nc_ctx - 1 - s, nc + nc_ctx - 1 - s), 1, ms[1])
        hf_ref[pl.ds(r0, L), :] += hk
        return (mf, mb)

    lax.fori_loop(0, nc, scan_body, (m0, m0))

    def finish_body(s, carry):
        r0 = pl.multiple_of(s * L, L)
        y = hf_ref[pl.ds(r0, L), :]
        y = y * lax.rsqrt(jnp.mean(y * y, axis=-1, keepdims=True) + NORM_EPS) * ng_ref[...]
        y = y * jax.nn.sigmoid(o_ref[0, pl.ds(r0, L), :].astype(f32))
        out_ref[0, pl.ds(r0, L), :] = y.astype(out_ref.dtype)
        return carry

    lax.fori_loop(0, nc, finish_body, 0)


def mlstm_mixer(qkv, o_src, o_off, gate_small, gate_b, norm_g, ctx):
    b, s, _ = qkv.shape
    nh = gate_b.shape[-1]
    vtot = norm_g.shape[-1]
    dv = vtot // nh
    dk = (qkv.shape[-1] - vtot) // (2 * nh)
    L = MLSTM_CHUNK
    nc, nc_ctx = s // L, ctx // L
    assert s % L == 0 and ctx % L == 0 and dk % LANE == 0 and dv % LANE == 0 and o_off % dv == 0
    assert 4 * nh <= LANE and gate_small.shape[-1] == LANE
    gt = chunk_transpose(gate_small, L)
    gb = jnp.pad(gate_b.astype(f32).reshape(4, nh).T, ((0, 0), (0, 4)))
    gb = jnp.broadcast_to(gb[:, :, None], (nh, 8, L))
    gb_c = jnp.pad(gate_b.astype(f32).reshape(1, 4 * nh), ((0, 0), (0, LANE - 4 * nh)))
    ng = norm_g.astype(f32)[None, :]
    koff = nh
    voff = 2 * nh * dk // dv
    ooff = o_off // dv
    return pl.pallas_call(
        functools.partial(_mlstm_kernel, nc_ctx=nc_ctx, nc=nc, dk=dk, dv=dv, n_heads=nh),
        grid=(b, nh),
        in_specs=[
            pl.BlockSpec((1, s, dk), lambda bi, hi: (bi, 0, hi)),
            pl.BlockSpec((1, s, dk), lambda bi, hi: (bi, 0, koff + hi)),
            pl.BlockSpec((1, s, dv), lambda bi, hi: (bi, 0, voff + hi)),
            pl.BlockSpec((1, s, dv), lambda bi, hi: (bi, 0, ooff + hi)),
            pl.BlockSpec((1, nc, LANE, L), lambda bi, hi: (bi, 0, 0, 0)),
            pl.BlockSpec((1, 8, L), lambda bi, hi: (hi, 0, 0)),
            pl.BlockSpec((1, s, LANE), lambda bi, hi: (bi, 0, 0)),
            pl.BlockSpec((1, LANE), lambda bi, hi: (0, 0)),
            pl.BlockSpec((1, dv), lambda bi, hi: (0, hi)),
        ],
        out_specs=pl.BlockSpec((1, s, dv), lambda bi, hi: (bi, 0, hi)),
        out_shape=jax.ShapeDtypeStruct((b, s, vtot), bf16),
        scratch_shapes=[pltpu.VMEM((s, dv), f32), pltpu.VMEM((2, dk, dv + LANE), f32)],
        compiler_params=_cparams(("parallel", "parallel")),
        name="mlstm_mixer",
    )(qkv, qkv, qkv, o_src, gt, gb, gate_small, gb_c, ng)


def _outproj_kernel(y_ref, w_ref, xl_ref, xc_ref, g_ref, ol_ref, oc_ref, *, n_ctx, cols, d):
    j = pl.program_id(1)
    acc = jnp.dot(y_ref[0], w_ref[...], preferred_element_type=f32)

    @pl.when(j < n_ctx)
    def _():
        oc_ref[0] = xc_ref[0] + g_ref[0, 0:1, :] * acc

    @pl.when(j >= n_ctx)
    def _():
        gl = g_ref[0, 1:2, :]
        if cols == 0:
            ol_ref[0] = xl_ref[0] + gl * acc
        else:
            nr = xl_ref.shape[1]
            for ci in range(cols):
                ol_ref[0, :, ci * d:(ci + 1) * d] = xl_ref[0, :, ci * d:(ci + 1) * d] + gl * acc[ci * nr:(ci + 1) * nr, :]


def outproj_residual(y, w, x_lat, x_ctx, gates, col_major):
    b, t, d = x_lat.shape
    ctx = x_ctx.shape[1]
    kin = y.shape[-1]
    tt = ROW_TILE
    n_ctx, n_lat = ctx // tt, t // tt
    if col_major:
        xl, xl_spec, cols = _col_view(x_lat, tt, n_ctx)
    else:
        cols = 0
        xl = x_lat
        xl_spec = pl.BlockSpec((1, tt, d), lambda bi, j: (bi, jnp.maximum(j - n_ctx, 0), 0))
    ol, oc = pl.pallas_call(
        functools.partial(_outproj_kernel, n_ctx=n_ctx, cols=cols, d=d),
        grid=(b, n_ctx + n_lat),
        in_specs=[
            pl.BlockSpec((1, tt, kin), lambda bi, j: (bi, j, 0)),
            pl.BlockSpec((kin, d), lambda bi, j: (0, 0), pipeline_mode=pl.Buffered(1)),
            xl_spec,
            pl.BlockSpec((1, tt, d), lambda bi, j: (bi, jnp.minimum(j, n_ctx - 1), 0)),
            pl.BlockSpec((1, 2, d), lambda bi, j: (bi, 0, 0)),
        ],
        out_specs=[xl_spec, pl.BlockSpec((1, tt, d), lambda bi, j: (bi, jnp.minimum(j, n_ctx - 1), 0))],
        out_shape=[jax.ShapeDtypeStruct(xl.shape, f32), jax.ShapeDtypeStruct(x_ctx.shape, f32)],
        compiler_params=_cparams(("parallel", "arbitrary")),
        name="outproj_residual",
    )(y, w, xl, x_ctx, gates)
    return ol.reshape(b, t, d), oc


def _mod_kernel(c_ref, w_ref, b_ref, o_ref):
    cvec = c_ref[...]
    cs = (cvec * jax.nn.sigmoid(cvec)).astype(bf16)
    o_ref[0] = jnp.dot(cs, w_ref[0].astype(bf16), preferred_element_type=f32) + b_ref[0]


def modulation(c_rows, mod_w, mod_b):
    depth, d, n = mod_w.shape
    tn = 1024
    assert n % tn == 0
    return pl.pallas_call(
        _mod_kernel,
        grid=(depth, n // tn),
        in_specs=[
            pl.BlockSpec(c_rows.shape, lambda i, j: (0, 0)),
            pl.BlockSpec((1, d, tn), lambda i, j: (i, 0, j)),
            pl.BlockSpec((1, 1, tn), lambda i, j: (i, 0, j)),
        ],
        out_specs=pl.BlockSpec((1, c_rows.shape[0], tn), lambda i, j: (i, 0, j)),
        out_shape=jax.ShapeDtypeStruct((depth, c_rows.shape[0], n), f32),
        compiler_params=_cparams(("parallel", "parallel")),
        name="modulation",
    )(c_rows, mod_w, mod_b.reshape(depth, 1, n))


def _moe_kernel(be_ref, nu_ref, x_ref, wgu_ref, bgu_ref, wd_ref, bd_ref, o_ref, *, f):
    i = pl.program_id(0)
    dh = x_ref.shape[1]

    @pl.when(i < nu_ref[0])
    def _():
        xa, xb = _unpack_pair(x_ref[...])
        h = (jnp.dot(xa.astype(bf16), wgu_ref[0, 0, :dh, :], preferred_element_type=f32)
             + jnp.dot(xb.astype(bf16), wgu_ref[0, 0, dh:, :], preferred_element_type=f32) + bgu_ref[0, 0])
        glu = jnp.minimum(h[:, :f], SWIGLU_LIMIT)
        lin = jnp.clip(h[:, f:], -SWIGLU_LIMIT, SWIGLU_LIMIT)
        act = glu * jax.nn.sigmoid(SWIGLU_ALPHA * glu) * (lin + 1.0)
        y = jnp.dot(act.astype(bf16), wd_ref[0, 0], preferred_element_type=f32) + bd_ref[0, 0]
        yb = y.astype(bf16).astype(f32)
        o_ref[...] = _pack_pair(yb[:, :dh], yb[:, dh:])

    @pl.when(i >= nu_ref[0])
    def _():
        o_ref[...] = jnp.zeros_like(o_ref)


def moe_experts(xs, block_e, n_used, layer, gu_w, gu_b, down_w, down_b, tm):
    n_slots, dh = xs.shape
    depth, e, d, f2 = gu_w.shape
    f = f2 // 2
    n_blocks = n_slots // tm
    last = lambda i, nu: jnp.minimum(i, nu[0] - 1)
    grid_spec = pltpu.PrefetchScalarGridSpec(
        num_scalar_prefetch=2,
        grid=(n_blocks,),
        in_specs=[
            pl.BlockSpec((tm, dh), lambda i, be, nu: (last(i, nu), 0)),
            pl.BlockSpec((1, 1, d, f2), lambda i, be, nu: (layer, be[last(i, nu)], 0, 0)),
            pl.BlockSpec((1, 1, 1, f2), lambda i, be, nu: (layer, be[last(i, nu)], 0, 0)),
            pl.BlockSpec((1, 1, f, d), lambda i, be, nu: (layer, be[last(i, nu)], 0, 0)),
            pl.BlockSpec((1, 1, 1, d), lambda i, be, nu: (layer, be[last(i, nu)], 0, 0)),
        ],
        out_specs=pl.BlockSpec((tm, dh), lambda i, be, nu: (i, 0)),
    )
    return pl.pallas_call(
        functools.partial(_moe_kernel, f=f),
        grid_spec=grid_spec,
        out_shape=jax.ShapeDtypeStruct((n_slots, dh), jnp.uint32),
        compiler_params=_cparams(("arbitrary",)),
        name="moe_experts",
    )(block_e, n_used, xs, gu_w, gu_b.reshape(depth, e, 1, f2), down_w, down_b.reshape(depth, e, 1, d))


ROUTE_TILE = 512


def _route_kernel(lg_ref, rb_ref, ri_ref, rg_ref, cnt_ref, run_ref, *, n_exp):
    i = pl.program_id(0)
    tb = lg_ref.shape[0]
    i32 = jnp.int32

    @pl.when(i == 0)
    def _():
        run_ref[...] = jnp.zeros_like(run_ref)

    lane = lax.broadcasted_iota(i32, (tb, LANE), 1)
    lg = jnp.where(lane < n_exp, lg_ref[...] + rb_ref[...], -jnp.inf)
    vals, idxs, hots = [], [], []
    for _ in range(TOP_K):
        m = jnp.max(lg, axis=1, keepdims=True)
        idx = jnp.min(jnp.where(lg == m, lane, LANE), axis=1, keepdims=True)
        hot = lane == idx
        lg = jnp.where(hot, -jnp.inf, lg)
        vals.append(m)
        idxs.append(idx)
        hots.append(hot)
    es = [jnp.exp(v - vals[0]) for v in vals]
    tot = es[0]
    for ev in es[1:]:
        tot = tot + ev
    msum = hots[0].astype(f32)
    for hot in hots[1:]:
        msum = msum + hot.astype(f32)
    tri = (lax.broadcasted_iota(i32, (tb, tb), 0) > lax.broadcasted_iota(i32, (tb, tb), 1)).astype(bf16)
    before = jnp.dot(tri, msum.astype(bf16), preferred_element_type=f32) + run_ref[...]
    ri = jnp.zeros((tb, LANE), i32)
    rg = jnp.zeros((tb, LANE), f32)
    for k in range(TOP_K):
        rank = jnp.sum(jnp.where(hots[k], before, 0.0), axis=1, keepdims=True)
        ri = jnp.where(lane == k, idxs[k], ri)
        ri = jnp.where(lane == TOP_K + k, rank.astype(i32), ri)
        rg = jnp.where(lane == k, es[k] / tot, rg)
    ri_ref[...] = ri
    rg_ref[...] = rg
    run_ref[...] = run_ref[...] + jnp.sum(msum, axis=0, keepdims=True)
    cnt_ref[...] = run_ref[...]


def route(logits, router_b, n_exp):
    n = logits.shape[0]
    tb = _pick_tile(n, (ROUTE_TILE, 256))
    return pl.pallas_call(
        functools.partial(_route_kernel, n_exp=n_exp),
        grid=(n // tb,),
        in_specs=[pl.BlockSpec((tb, LANE), lambda i: (i, 0)), pl.BlockSpec((1, LANE), lambda i: (0, 0))],
        out_specs=[pl.BlockSpec((tb, LANE), lambda i: (i, 0)), pl.BlockSpec((tb, LANE), lambda i: (i, 0)),
                   pl.BlockSpec((1, LANE), lambda i: (0, 0))],
        out_shape=[jax.ShapeDtypeStruct((n, LANE), jnp.int32), jax.ShapeDtypeStruct((n, LANE), f32),
                   jax.ShapeDtypeStruct((1, LANE), f32)],
        scratch_shapes=[pltpu.VMEM((1, LANE), f32)],
        compiler_params=_cparams(("arbitrary",)),
        name="route",
    )(logits, router_b)


def _row_copy(src, s_row, dst, d_row, sem):
    return pltpu.make_async_copy(src.at[pl.ds(s_row, 1)], dst.at[pl.ds(d_row, 1)], sem)


def _dispatch_kernel(zs_ref, zc_ref, nu_ref, slot_ref, u_ref, xs_ref, zero_ref, sem, *, n_exp, tm):
    i = pl.program_id(0)
    tb = u_ref.shape[0]
    n_blocks = xs_ref.shape[0] // tm

    @pl.when(i == 0)
    def _():
        zero_ref[...] = jnp.zeros_like(zero_ref)

        def block_copy(blk):
            return pltpu.make_async_copy(zero_ref, xs_ref.at[pl.ds(pl.multiple_of(blk * tm, tm), tm)], sem)

        def start_block(blk, carry):
            block_copy(blk).start()
            return carry

        def wait_block(blk, carry):
            block_copy(blk).wait()
            return carry

        lax.fori_loop(nu_ref[0], n_blocks, start_block, 0)
        lax.fori_loop(nu_ref[0], n_blocks, wait_block, 0)

        def per_expert(e, carry):
            def start(r, c):
                _row_copy(zero_ref, 0, xs_ref, zs_ref[e] + r, sem).start()
                return c

            def wait(r, c):
                _row_copy(zero_ref, 0, xs_ref, 0, sem).wait()
                return c

            lax.fori_loop(0, zc_ref[e], start, 0)
            lax.fori_loop(0, zc_ref[e], wait, 0)
            return carry

        lax.fori_loop(0, n_exp, per_expert, 0)

    def start(r, c):
        for k in range(TOP_K):
            _row_copy(u_ref, r, xs_ref, slot_ref[0, 0, r * TOP_K + k], sem).start(priority=k % 2)
        return c

    def wait(r, c):
        _row_copy(u_ref, 0, xs_ref, 0, sem).wait()
        return c

    lax.fori_loop(0, tb, start, 0, unroll=4)
    lax.fori_loop(0, tb * TOP_K, wait, 0, unroll=8)


def dispatch(u, slot, zero_start, zero_count, n_used, n_blocks, tm):
    n, w = u.shape
    tb = ROW_TILE
    n_exp = zero_start.shape[0]
    grid_spec = pltpu.PrefetchScalarGridSpec(
        num_scalar_prefetch=3,
        grid=(n // tb,),
        in_specs=[
            pl.BlockSpec((1, 1, tb * TOP_K), lambda i, zs, zc, nu: (i, 0, 0), memory_space=pltpu.SMEM),
            pl.BlockSpec((tb, w), lambda i, zs, zc, nu: (i, 0)),
        ],
        out_specs=pl.BlockSpec(memory_space=pl.ANY),
        scratch_shapes=[pltpu.VMEM((tm, w), jnp.uint32), pltpu.SemaphoreType.DMA],
    )
    return pl.pallas_call(
        functools.partial(_dispatch_kernel, n_exp=n_exp, tm=tm),
        grid_spec=grid_spec,
        out_shape=jax.ShapeDtypeStruct((n_blocks * tm, w), jnp.uint32),
        compiler_params=_cparams(("arbitrary",)),
        name="dispatch",
    )(zero_start, zero_count, n_used, slot.reshape(n // tb, 1, tb * TOP_K), u)


def _combine_kernel(slot_ref, rg_ref, xl_ref, xc_ref, g_ref, ys_ref, ol_ref, oc_ref, buf_ref, sem, *, n_ctx):
    j = pl.program_id(1)
    tb = rg_ref.shape[0]

    def start(r, c):
        for k in range(TOP_K):
            pltpu.make_async_copy(ys_ref.at[pl.ds(slot_ref[0, 0, r * TOP_K + k], 1)],
                                  buf_ref.at[k, pl.ds(r, 1)], sem).start(priority=k % 2)
        return c

    def wait(r, c):
        pltpu.make_async_copy(ys_ref.at[pl.ds(0, 1)], buf_ref.at[0, pl.ds(0, 1)], sem).wait()
        return c

    lax.fori_loop(0, tb, start, 0, unroll=4)
    lax.fori_loop(0, tb * TOP_K, wait, 0, unroll=8)

    gates = rg_ref[...]
    acc = None
    for k in range(TOP_K):
        ya, yb = _unpack_pair(buf_ref[k])
        term = gates[:, k:k + 1] * jnp.concatenate([ya, yb], axis=1)
        acc = term if acc is None else acc + term

    @pl.when(j < n_ctx)
    def _():
        oc_ref[0] = xc_ref[0] + g_ref[0, 0:1, :] * acc

    @pl.when(j >= n_ctx)
    def _():
        ol_ref[0] = xl_ref[0] + g_ref[0, 1:2, :] * acc


def combine(ys, slot, rg, x_lat, x_ctx, gates):
    b, t, d = x_lat.shape
    ctx = x_ctx.shape[1]
    tb = ROW_TILE
    n_ctx, n_lat = ctx // tb, t // tb
    steps = n_ctx + n_lat
    return pl.pallas_call(
        functools.partial(_combine_kernel, n_ctx=n_ctx),
        grid=(b, steps),
        in_specs=[
            pl.BlockSpec((1, 1, tb * TOP_K), lambda bi, j: (bi * steps + j, 0, 0), memory_space=pltpu.SMEM),
            pl.BlockSpec((tb, LANE), lambda bi, j: (bi * steps + j, 0)),
            pl.BlockSpec((1, tb, d), lambda bi, j: (bi, jnp.maximum(j - n_ctx, 0), 0)),
            pl.BlockSpec((1, tb, d), lambda bi, j: (bi, jnp.minimum(j, n_ctx - 1), 0)),
            pl.BlockSpec((1, 2, d), lambda bi, j: (bi, 0, 0)),
            pl.BlockSpec(memory_space=pl.ANY),
        ],
        out_specs=[pl.BlockSpec((1, tb, d), lambda bi, j: (bi, jnp.maximum(j - n_ctx, 0), 0)),
                   pl.BlockSpec((1, tb, d), lambda bi, j: (bi, jnp.minimum(j, n_ctx - 1), 0))],
        out_shape=[jax.ShapeDtypeStruct(x_lat.shape, f32), jax.ShapeDtypeStruct(x_ctx.shape, f32)],
        scratch_shapes=[pltpu.VMEM((TOP_K, tb, d // 2), jnp.uint32), pltpu.SemaphoreType.DMA],
        compiler_params=_cparams(("arbitrary", "arbitrary")),
        name="combine",
    )(slot.reshape(b * steps, 1, tb * TOP_K), rg, x_lat, x_ctx, gates, ys)


def _final_norm_kernel(x_ref, g_ref, o_ref):
    x = x_ref[0]
    o_ref[0] = (x * lax.rsqrt(jnp.mean(x * x, axis=-1, keepdims=True) + NORM_EPS)) * g_ref[...]


def final_norm(x, g):
    b, t, d = x.shape
    tt = 512 if t % 512 == 0 else ROW_TILE
    return pl.pallas_call(
        _final_norm_kernel,
        grid=(b, t // tt),
        in_specs=[pl.BlockSpec((1, tt, d), lambda bi, j: (bi, j, 0)), pl.BlockSpec((1, d), lambda bi, j: (0, 0))],
        out_specs=pl.BlockSpec((1, tt, d), lambda bi, j: (bi, j, 0)),
        out_shape=jax.ShapeDtypeStruct((b, t, d), f32),
        compiler_params=_cparams(("parallel", "parallel")),
        name="final_norm",
    )(x, g)


MOE_TILE = 512


def _pick_tile(n, prefs):
    for p in prefs:
        if n % p == 0:
            return p
    raise ValueError(f"no tile in {prefs} divides {n}")


def _pad_lanes(w):
    return jnp.pad(w, ((0, 0), (0, LANE - w.shape[1])))


def _project(h, w_main):
    b, s, d = h.shape
    m = b * s
    n = w_main.shape[1]
    out = matmul(h.reshape(m, d), w_main, bf16, _pick_tile(m, (1024, 512, 256)), _pick_tile(n, (1024, 512, 256, 128)))
    return out.reshape(b, s, n)


def _moe_ffn(u, logits, router_b, layer, gu_w, gu_b, down_w, down_b, x_lat, x_ctx, gates):
    n = u.shape[0]
    e = gu_w.shape[1]
    tm = MOE_TILE
    i32 = jnp.int32
    ri, rg, counts = route(logits, _pad_lanes(router_b.astype(f32)[None, :]), e)
    cnt = counts[0, :e].astype(i32)
    padded = (cnt + tm - 1) // tm * tm
    pad_end = jnp.cumsum(padded)
    pad_start = pad_end - padded
    n_blocks = -(-(n * TOP_K) // tm) + e
    top_idx, rank = ri[:, :TOP_K], ri[:, TOP_K:2 * TOP_K]
    hot = top_idx[:, :, None] == jnp.arange(e, dtype=i32)[None, None, :]
    slot = (jnp.sum(jnp.where(hot, pad_start[None, None, :], 0), axis=-1) + rank).reshape(-1)
    block_start = jnp.arange(n_blocks, dtype=i32) * tm
    block_e = jnp.minimum(jnp.sum((block_start[:, None] >= pad_end[None, :]).astype(i32), axis=1), e - 1)
    n_used = (pad_end[-1] // tm).astype(i32).reshape(1)
    xs = dispatch(u, slot, pad_start + cnt, padded - cnt, n_used, n_blocks, tm)
    ys = moe_experts(xs, block_e, n_used, layer, gu_w, gu_b, down_w, down_b, tm)
    return combine(ys, slot, rg, x_lat, x_ctx, gates)


def kernel(x, c, ctx, c_ctx, mod_w, mod_b, norm1_g, norm2_g, final_g, m_in_w, m_conv_w, m_conv_b, m_dt_bias, m_a_log, m_d, m_norm_g, m_out_w, l_in_w, l_conv_w, l_conv_b, l_gate_b, l_norm_g, l_out_w, router_w, router_b, e_gu_w, e_gu_b, e_down_w, e_down_b):
    b, t, d = x.shape
    n_ctx_tok = ctx.shape[1]
    depth = mod_w.shape[0]
    m_inner = m_out_w.shape[1]
    m_conv = m_conv_w.shape[-1]
    l_qkv = l_conv_w.shape[-1]
    l_vtot = l_out_w.shape[1]
    gu_w, down_w = e_gu_w.astype(bf16), e_down_w.astype(bf16)

    c_rows = jnp.zeros((16, d), f32).at[:b].set(c).at[b].set(c_ctx)
    mod_all = modulation(c_rows, mod_w, mod_b)

    xl, xc = x, ctx
    for i in range(depth):
        j = i // 2
        mod = mod_all[i].reshape(16, 6, d)
        lat, cx = mod[:b], jnp.broadcast_to(mod[b][None], (b, 6, d))

        def mods(k):
            return jnp.stack([cx[:, k], cx[:, k + 1], lat[:, k], lat[:, k + 1]], axis=1)

        def gate(k):
            return jnp.stack([cx[:, k], lat[:, k]], axis=1)

        if i % 2 == 0:
            w_in = m_in_w[j].astype(bf16)
            n_main = m_inner + m_conv
            h, dt_small = normmod(xl, xc, norm1_g[i][None], mods(0), _pad_lanes(w_in[:, n_main:]), False)
            proj = _project(h, w_in[:, :n_main])
            xbc = conv_silu(proj, m_conv_w[j], m_conv_b[j][None], m_inner, 512, n_ctx_tok)
            y = ssd_mixer(xbc, proj, dt_small, m_dt_bias[j], m_a_log[j], m_d[j], m_norm_g[j], n_ctx_tok)
            xl, xc = outproj_residual(y, m_out_w[j].astype(bf16), xl, xc, gate(2), False)
        else:
            w_in = l_in_w[j].astype(bf16)
            n_main = l_qkv + l_vtot
            h, gate_small = normmod(xl, xc, norm1_g[i][None], mods(0), _pad_lanes(w_in[:, n_main:]), True)
            proj = _project(h, w_in[:, :n_main])
            qkv = conv_silu(proj, l_conv_w[j], l_conv_b[j][None], 0, 512, n_ctx_tok)
            y = mlstm_mixer(qkv, proj, l_qkv, gate_small, l_gate_b[j], l_norm_g[j], n_ctx_tok)
            xl, xc = outproj_residual(y, l_out_w[j].astype(bf16), xl, xc, gate(2), True)

        u, logits = normmod(xl, xc, norm2_g[i][None], mods(3), _pad_lanes(router_w[i].astype(bf16)), False, pack=True)
        s = n_ctx_tok + t
        xl, xc = _moe_ffn(u.reshape(b * s, d // 2), logits.reshape(b * s, LANE), router_b[i], i, gu_w, e_gu_b,
                          down_w, e_down_b, xl, xc, gate(5))
    return final_norm(xl, final_g[None])
```

```python
import functools

import jax
import jax.numpy as jnp
import numpy as np
from jax import lax
from jax.experimental import pallas as pl
from jax.experimental.pallas import tpu as pltpu

f32 = jnp.float32
bf16 = jnp.bfloat16

NORM_EPS = 1e-6
CONV_W = 5
GRID_W = 64
M_D_STATE = 128
TOP_K = 4
SWIGLU_LIMIT = 7.0
SWIGLU_ALPHA = 1.702
SSD_CHUNK = 128
MLSTM_CHUNK = 256
ROW_TILE = 256
LANE = 128
V7X_VMEM_LIMIT = 60000 * 1024


def _cparams(sem):
    return pltpu.CompilerParams(dimension_semantics=sem, vmem_limit_bytes=V7X_VMEM_LIMIT)


def _pack_pair(a, b):
    u32 = jnp.uint32
    return (lax.bitcast_convert_type(a, u32) & u32(0xFFFF0000)) | (lax.bitcast_convert_type(b, u32) >> 16)


def _unpack_pair(w):
    u32 = jnp.uint32
    return (lax.bitcast_convert_type(w & u32(0xFFFF0000), f32), lax.bitcast_convert_type(w << 16, f32))


def _col_view(x_lat, tt, n_ctx):
    b, t, d = x_lat.shape
    r = t // GRID_W
    cols = tt // r
    assert tt % r == 0 and GRID_W % cols == 0
    spec = pl.BlockSpec((1, r, cols * d), lambda bi, j: (bi, 0, jnp.maximum(j - n_ctx, 0)))
    return x_lat.reshape(b, r, GRID_W * d), spec, cols


def _normmod_kernel(xl_ref, xc_ref, g_ref, mod_ref, ws_ref, h_ref, small_ref, *, n_ctx, cols, d, pack):
    j = pl.program_id(1)

    def rows(x, shift, scale, r0, nr):
        ms = jnp.mean(x * x, axis=-1, keepdims=True)
        y = (x * lax.rsqrt(ms + NORM_EPS)) * g_ref[...]
        hb = (y * (1.0 + scale) + shift).astype(bf16)
        if pack:
            hf = hb.astype(f32)
            h_ref[0, r0:r0 + nr, :] = _pack_pair(hf[:, :d // 2], hf[:, d // 2:])
        else:
            h_ref[0, r0:r0 + nr, :] = hb
        small_ref[0, r0:r0 + nr, :] = jnp.dot(hb, ws_ref[...], preferred_element_type=f32)

    @pl.when(j < n_ctx)
    def _():
        rows(xc_ref[0], mod_ref[0, 0:1, :], mod_ref[0, 1:2, :], 0, xc_ref.shape[1])

    @pl.when(j >= n_ctx)
    def _():
        if cols == 0:
            rows(xl_ref[0], mod_ref[0, 2:3, :], mod_ref[0, 3:4, :], 0, xl_ref.shape[1])
        else:
            nr = xl_ref.shape[1]
            for ci in range(cols):
                rows(xl_ref[0, :, ci * d:(ci + 1) * d], mod_ref[0, 2:3, :], mod_ref[0, 3:4, :], ci * nr, nr)


def normmod(x_lat, x_ctx, g, mods, w_small, col_major, pack=False):
    b, t, d = x_lat.shape
    ctx = x_ctx.shape[1]
    tt = ROW_TILE
    n_ctx, n_lat = ctx // tt, t // tt
    assert ctx % tt == 0 and t % tt == 0
    if col_major:
        xl, xl_spec, cols = _col_view(x_lat, tt, n_ctx)
    else:
        cols = 0
        xl = x_lat
        xl_spec = pl.BlockSpec((1, tt, d), lambda bi, j: (bi, jnp.maximum(j - n_ctx, 0), 0))
    s = ctx + t
    dh = d // 2 if pack else d
    return pl.pallas_call(
        functools.partial(_normmod_kernel, n_ctx=n_ctx, cols=cols, d=d, pack=pack),
        grid=(b, n_ctx + n_lat),
        in_specs=[
            xl_spec,
            pl.BlockSpec((1, tt, d), lambda bi, j: (bi, jnp.minimum(j, n_ctx - 1), 0)),
            pl.BlockSpec((1, d), lambda bi, j: (0, 0)),
            pl.BlockSpec((1, 4, d), lambda bi, j: (bi, 0, 0)),
            pl.BlockSpec((d, LANE), lambda bi, j: (0, 0)),
        ],
        out_specs=[
            pl.BlockSpec((1, tt, dh), lambda bi, j: (bi, j, 0)),
            pl.BlockSpec((1, tt, LANE), lambda bi, j: (bi, j, 0)),
        ],
        out_shape=[jax.ShapeDtypeStruct((b, s, dh), jnp.uint32 if pack else bf16),
                   jax.ShapeDtypeStruct((b, s, LANE), f32)],
        compiler_params=_cparams(("parallel", "arbitrary")),
        name="normmod",
    )(xl, x_ctx, g, mods, w_small)


def _mm_kernel(x_ref, w_ref, o_ref):
    o_ref[...] = jnp.dot(x_ref[...], w_ref[...], preferred_element_type=f32).astype(o_ref.dtype)


def matmul(x, w, out_dtype, tm, tn):
    m, k = x.shape
    n = w.shape[1]
    assert m % tm == 0 and n % tn == 0
    return pl.pallas_call(
        _mm_kernel,
        grid=(m // tm, n // tn),
        in_specs=[pl.BlockSpec((tm, k), lambda i, j: (i, 0)), pl.BlockSpec((k, tn), lambda i, j: (0, j))],
        out_specs=pl.BlockSpec((tm, tn), lambda i, j: (i, j)),
        out_shape=jax.ShapeDtypeStruct((m, n), out_dtype),
        compiler_params=_cparams(("parallel", "arbitrary")),
        name="matmul",
    )(x, w)


def _conv_kernel(u_ref, w_ref, b_ref, o_ref, pad_ref, *, ctx, tile):
    s, cb = u_ref.shape[1], u_ref.shape[2]
    half = CONV_W // 2
    zero8 = jnp.zeros((8, cb), f32)
    pad_ref[0:8, :] = zero8
    pad_ref[8 + ctx:16 + ctx, :] = zero8
    pad_ref[16 + s:24 + s, :] = zero8
    for r0 in range(0, s, tile):
        base = 8 if r0 < ctx else 16
        pad_ref[base + r0:base + r0 + tile, :] = u_ref[0, r0:r0 + tile, :].astype(f32)
    w = w_ref[...]
    bias = b_ref[...]
    for r0 in range(0, s, tile):
        base = 8 if r0 < ctx else 16
        big = pad_ref[base + r0 - 8:base + r0 + tile + 8, :]
        acc = bias + w[half:half + 1, :] * big[8:8 + tile, :]
        for k in range(CONV_W):
            if k != half:
                rolled = pltpu.roll(big, (half - k) % (tile + 16), axis=0)
                acc = acc + w[k:k + 1, :] * rolled[8:8 + tile, :]
        o_ref[0, r0:r0 + tile, :] = (acc * jax.nn.sigmoid(acc)).astype(o_ref.dtype)


def conv_silu(u, w, bias, lane_off, cb, ctx):
    b, s, _ = u.shape
    c = w.shape[1]
    assert c % cb == 0 and lane_off % cb == 0
    tile = ROW_TILE
    assert ctx % tile == 0 and s % tile == 0
    off = lane_off // cb
    return pl.pallas_call(
        functools.partial(_conv_kernel, ctx=ctx, tile=tile),
        grid=(b, c // cb),
        in_specs=[
            pl.BlockSpec((1, s, cb), lambda bi, j: (bi, 0, off + j)),
            pl.BlockSpec((CONV_W, cb), lambda bi, j: (0, j)),
            pl.BlockSpec((1, cb), lambda bi, j: (0, j)),
        ],
        out_specs=pl.BlockSpec((1, s, cb), lambda bi, j: (bi, 0, j)),
        out_shape=jax.ShapeDtypeStruct((b, s, c), bf16),
        scratch_shapes=[pltpu.VMEM((s + 24, cb), f32)],
        compiler_params=_cparams(("parallel", "parallel")),
        name="conv_silu",
    )(u, w, bias)


def _split3(a):
    hi = a.astype(bf16)
    r1 = a - hi.astype(f32)
    mid = r1.astype(bf16)
    lo = (r1 - mid.astype(f32)).astype(bf16)
    return jnp.concatenate([hi, mid, lo], axis=0)


def _softplus(x):
    return jnp.maximum(x, 0.0) + jnp.log1p(jnp.exp(-jnp.abs(x)))


def _dot_t(a, b):
    return lax.dot_general(a, b, (((0,), (0,)), ((), ())), preferred_element_type=f32)


def _dot_nt(a, b):
    return lax.dot_general(a, b, (((1,), (1,)), ((), ())), preferred_element_type=f32)


def _chunk_transpose_kernel(x_ref, o_ref):
    nc, _, chunk = o_ref.shape[1:]
    for ci in range(nc):
        o_ref[0, ci] = x_ref[0, ci * chunk:(ci + 1) * chunk, :].T


def chunk_transpose(x, chunk):
    b, s, w = x.shape
    nc = s // chunk
    return pl.pallas_call(
        _chunk_transpose_kernel,
        grid=(b,),
        in_specs=[pl.BlockSpec((1, s, w), lambda bi: (bi, 0, 0))],
        out_specs=pl.BlockSpec((1, nc, w, chunk), lambda bi: (bi, 0, 0, 0)),
        out_shape=jax.ShapeDtypeStruct((b, nc, w, chunk), f32),
        compiler_params=_cparams(("parallel",)),
        name="chunk_transpose",
    )(x)


def _hi_mid(v):
    hi = v.astype(bf16)
    return jnp.concatenate([hi, (v - hi.astype(f32)).astype(bf16)], axis=1)


def _ssd_kernel(x_ref, b_ref, c_ref, z_ref, dt_ref, bias_ref, a_ref, dtc_ref, biasc_ref, ac_ref, dskip_ref, ng_ref,
                o_ref, yf_ref, h_ref, earg_ref, echan_ref, *, nc_ctx, nc, heads, headdim, n_heads):
    L = SSD_CHUNK
    R, P = heads, headdim
    RP = R * P
    i32 = jnp.int32
    gi = pl.program_id(1)

    row = lax.broadcasted_iota(i32, (L, L), 0)
    col = lax.broadcasted_iota(i32, (L, L), 1)
    lane_lo = lax.broadcasted_iota(i32, (L, 2 * P), 1) < P
    tri_c = jnp.tile((col <= row).astype(bf16), (1, 2))

    for d in range(2):
        first = d * n_heads + gi * R
        er = lax.broadcasted_iota(i32, (2 * LANE, R * L), 0) % LANE
        ec = lax.broadcasted_iota(i32, (2 * LANE, R * L), 1) // L
        earg_ref[d] = (er == first + ec).astype(bf16)
        fr = lax.broadcasted_iota(i32, (2 * LANE, RP), 0) % LANE
        fc = lax.broadcasted_iota(i32, (2 * LANE, RP), 1) // P
        echan_ref[d] = (fr == first + fc).astype(bf16)

    def chunk(ci, d):
        fwd = d == 0
        tri = ((row <= col) if fwd else (row >= col)).astype(bf16)
        mask = (col <= row) if fwd else (col >= row)
        r0 = pl.multiple_of(ci * L, L)
        raw = dt_ref[0, ci, pl.ds(pl.multiple_of(d * n_heads + gi * R, 8), R), :]
        dt = _softplus(raw + bias_ref[0, d * R:(d + 1) * R, :])
        a = dt * a_ref[0, d * R:(d + 1) * R, :]
        cum3 = jnp.dot(_split3(a), tri, preferred_element_type=f32)
        cum = cum3[0:R] + cum3[R:2 * R] + cum3[2 * R:3 * R]
        srow = cum - jnp.log(jnp.maximum(dt, 1e-37))
        crow = jnp.concatenate([srow[h:h + 1, :] for h in range(R)], axis=1)
        dtc = _softplus(dtc_ref[0, pl.ds(r0, L), :] + biasc_ref[...])
        ac = dtc * ac_ref[...]
        ac_hi = ac.astype(bf16)
        ac_parts = jnp.concatenate([ac_hi, (ac - ac_hi.astype(f32)).astype(bf16)], axis=0)
        pre = jnp.dot(tri_c, ac_parts, preferred_element_type=f32)
        tot = pre[L - 1:L, :]
        cumc = pre if fwd else tot - pre + ac
        arg = jnp.dot(_hi_mid(cumc), earg_ref[d], preferred_element_type=f32) - crow
        ecum_x = jnp.dot(_hi_mid(jnp.exp(cumc)), echan_ref[d], preferred_element_type=f32)
        ws_x = jnp.dot(_hi_mid(jnp.exp(tot - cumc) * dtc), echan_ref[d], preferred_element_type=f32)
        etot_x = ecum_x[L - 1:L, :] if fwd else ecum_x[0:1, :]

        xc = x_ref[0, pl.ds(r0, L), :]
        bc = b_ref[0, pl.ds(r0, L), :]
        cc = c_ref[0, pl.ds(r0, L), :]
        cb = _dot_nt(cc, bc)
        ys = []
        for p in range(R // 2):
            w_pair = []
            for h in (2 * p, 2 * p + 1):
                dec = jnp.exp(jnp.where(mask, arg[:, h * L:(h + 1) * L], -jnp.inf))
                w_pair.append((dec * cb).astype(bf16))
            xp = xc[:, 2 * p * P:2 * (p + 1) * P]
            zero = jnp.zeros_like(xp)
            rhs = jnp.concatenate([jnp.where(lane_lo, xp, zero), jnp.where(lane_lo, zero, xp)], axis=0)
            ys.append(jnp.dot(jnp.concatenate(w_pair, axis=1), rhs, preferred_element_type=f32))
        y_intra = jnp.concatenate(ys, axis=1)
        ht = h_ref[d]
        y = y_intra + jnp.dot(cc, ht.astype(bf16), preferred_element_type=f32) * ecum_x
        xw = (xc.astype(f32) * ws_x).astype(bf16)
        h_ref[d] = ht * etot_x + _dot_t(bc, xw)
        return r0, y

    h_ref[...] = jnp.zeros_like(h_ref)
    yf_ref[...] = jnp.zeros_like(yf_ref)

    def scan_body(s, carry):
        r0, y = chunk(s, 0)
        yf_ref[pl.ds(r0, L), :] += y
        r0, y = chunk(jnp.where(s < nc_ctx, nc_ctx - 1 - s, nc + nc_ctx - 1 - s), 1)
        yf_ref[pl.ds(r0, L), :] += y
        return carry

    lax.fori_loop(0, nc, scan_body, 0)

    def finish_body(s, carry):
        r0 = pl.multiple_of(s * L, L)
        y = yf_ref[pl.ds(r0, L), :] + dskip_ref[...] * x_ref[0, pl.ds(r0, L), :].astype(f32)
        z = z_ref[0, pl.ds(r0, L), :].astype(f32)
        y = y * (z * jax.nn.sigmoid(z))
        y = y * lax.rsqrt(jnp.mean(y * y, axis=-1, keepdims=True) + NORM_EPS)
        o_ref[0, pl.ds(r0, L), :] = (y * ng_ref[...]).astype(o_ref.dtype)
        return carry

    lax.fori_loop(0, nc, finish_body, 0)


def ssd_mixer(xbc, z_src, dt_small, dt_bias, a_log, d_skip, norm_g, ctx):
    b, s, _ = xbc.shape
    n = M_D_STATE
    nh = a_log.shape[-1]
    inner = norm_g.shape[-1]
    p = inner // nh
    g = (xbc.shape[-1] - inner) // (2 * n)
    r = nh // g
    rp = r * p
    L = SSD_CHUNK
    nc, nc_ctx = s // L, ctx // L
    assert p * 2 == LANE and s % L == 0 and ctx % L == 0 and r % 8 == 0 and rp % LANE == 0
    dt_t = chunk_transpose(dt_small, L)

    def per_head(v):
        v = v.astype(f32).reshape(2, g, r).transpose(1, 0, 2).reshape(g, 2 * r, 1)
        return jnp.broadcast_to(v, (g, 2 * r, L))

    bias_t = per_head(dt_bias)
    neg_a = -jnp.exp(a_log.astype(f32))
    a_t = per_head(neg_a)
    assert 2 * nh <= LANE and dt_small.shape[-1] == LANE
    bias_c = jnp.pad(dt_bias.astype(f32).reshape(1, 2 * nh), ((0, 0), (0, LANE - 2 * nh)))
    a_c = jnp.pad(neg_a.reshape(1, 2 * nh), ((0, 0), (0, LANE - 2 * nh)))
    dskip = jnp.repeat(d_skip.astype(f32), p)[None, :]
    ng = norm_g.astype(f32)[None, :]
    boff = inner // n
    return pl.pallas_call(
        functools.partial(_ssd_kernel, nc_ctx=nc_ctx, nc=nc, heads=r, headdim=p, n_heads=nh),
        grid=(b, g),
        in_specs=[
            pl.BlockSpec((1, s, rp), lambda bi, gi: (bi, 0, gi)),
            pl.BlockSpec((1, s, n), lambda bi, gi: (bi, 0, boff + gi)),
            pl.BlockSpec((1, s, n), lambda bi, gi: (bi, 0, boff + g + gi)),
            pl.BlockSpec((1, s, rp), lambda bi, gi: (bi, 0, gi)),
            pl.BlockSpec((1, nc, LANE, L), lambda bi, gi: (bi, 0, 0, 0)),
            pl.BlockSpec((1, 2 * r, L), lambda bi, gi: (gi, 0, 0)),
            pl.BlockSpec((1, 2 * r, L), lambda bi, gi: (gi, 0, 0)),
            pl.BlockSpec((1, s, LANE), lambda bi, gi: (bi, 0, 0)),
            pl.BlockSpec((1, LANE), lambda bi, gi: (0, 0)),
            pl.BlockSpec((1, LANE), lambda bi, gi: (0, 0)),
            pl.BlockSpec((1, rp), lambda bi, gi: (0, gi)),
            pl.BlockSpec((1, rp), lambda bi, gi: (0, gi)),
        ],
        out_specs=pl.BlockSpec((1, s, rp), lambda bi, gi: (bi, 0, gi)),
        out_shape=jax.ShapeDtypeStruct((b, s, inner), bf16),
        scratch_shapes=[pltpu.VMEM((s, rp), f32), pltpu.VMEM((2, n, rp), f32),
                        pltpu.VMEM((2, 2 * LANE, r * L), bf16), pltpu.VMEM((2, 2 * LANE, rp), bf16)],
        compiler_params=_cparams(("parallel", "parallel")),
        name="ssd_mixer",
    )(xbc, xbc, xbc, z_src, dt_t, bias_t, a_t, dt_small, bias_c, a_c, dskip, ng)


def _stack_rows(rows, n):
    w = next(r for r in rows if r is not None).shape[1]
    rid = lax.broadcasted_iota(jnp.int32, (n, w), 0)
    out = jnp.zeros((n, w), f32)
    for i, r in enumerate(rows):
        if r is not None:
            out = jnp.where(rid == i, jnp.broadcast_to(r, (n, w)), out)
    return out


def _parts3(a):
    hi = a.astype(bf16).astype(f32)
    mid = (a - hi).astype(bf16).astype(f32)
    lo = a - hi - mid
    return [hi, mid, lo]


def _mlstm_kernel(q_ref, k_ref, v_ref, o_ref, g_ref, gb_ref, gc_ref, gbc_ref, ng_ref, out_ref, hf_ref, c_ref,
                  *, nc_ctx, nc, dk, dv, n_heads):
    L = MLSTM_CHUNK
    i32 = jnp.int32
    scale = dk ** -0.5
    log_scale = -0.5 * float(np.log(dk))
    hi_ = pl.program_id(1)
    row = lax.broadcasted_iota(i32, (L, L), 0)
    col = lax.broadcasted_iota(i32, (L, L), 1)
    lane_e = lax.broadcasted_iota(i32, (L, LANE), 1)
    tri_c = jnp.tile((col <= row).astype(bf16), (1, 3))

    def chunk(ci, d, m):
        fwd = d == 0
        tri = ((row <= col) if fwd else (row >= col)).astype(bf16)
        mask = (col <= row) if fwd else (col >= row)
        r0 = pl.multiple_of(ci * L, L)
        lane_i = (2 * d) * n_heads + hi_
        ip = g_ref[0, ci, pl.ds(lane_i, 1), :] + gb_ref[0, 2 * d:2 * d + 1, :]
        logf = -_softplus(-(g_ref[0, ci, pl.ds(lane_i + n_heads, 1), :] + gb_ref[0, 2 * d + 1:2 * d + 2, :]))
        cum3 = jnp.dot(_stack_rows(_parts3(logf), 16).astype(bf16), tri, preferred_element_type=f32)
        bb = cum3[0:1] + cum3[1:2] + cum3[2:3]
        btot = bb[:, L - 1:L] if fwd else bb[:, 0:1]
        u = ip - bb
        gcol = gc_ref[0, pl.ds(r0, L), :] + gbc_ref[...]
        lfc = -_softplus(-gcol)
        pre = jnp.dot(tri_c, _split3(lfc), preferred_element_type=f32)
        cumc = pre if fwd else pre[L - 1:L, :] - pre + lfc
        b_col = jnp.sum(jnp.where(lane_e == lane_i + n_heads, cumc, 0.0), axis=1, keepdims=True)
        u_col = jnp.sum(jnp.where(lane_e == lane_i, gcol, 0.0), axis=1, keepdims=True) - b_col
        um = jnp.where(mask, u, -jnp.inf)
        mu = jnp.maximum(m, jnp.max(um, axis=1, keepdims=True))
        m_out = b_col + mu
        q = q_ref[0, pl.ds(r0, L), :]
        k = k_ref[0, pl.ds(r0, L), :]
        v = v_ref[0, pl.ds(r0, L), :]
        w = jnp.exp(um - (mu - log_scale)) * _dot_nt(q, k)
        inter_w = jnp.exp(m - mu) * scale
        cx = c_ref[d]
        qc = jnp.dot(q, cx.astype(bf16), preferred_element_type=f32)
        num = jnp.dot(w.astype(bf16), v, preferred_element_type=f32) + inter_w * qc[:, :dv]
        den = jnp.sum(w, axis=1, keepdims=True) + inter_w * (qc[:, dv:dv + 1] + qc[:, dv + 1:dv + 2])
        hk = num / jnp.maximum(jnp.abs(den), jnp.exp(-m_out))
        m_new = jnp.maximum(btot + m, btot + jnp.max(u, axis=1, keepdims=True))
        e_col = jnp.exp(u_col + btot - m_new)
        e_hi = e_col.astype(bf16).astype(f32)
        e_ext = jnp.where(lane_e == 0, e_hi, jnp.where(lane_e == 1, e_col - e_hi, 0.0))
        vext = jnp.concatenate([v.astype(f32) * e_col, e_ext], axis=1).astype(bf16)
        c_ref[d] = jnp.exp(btot + m - m_new) * cx + _dot_t(k, vext)
        return r0, hk, m_new

    m0 = jnp.full((1, 1), -jnp.inf, f32)
    c_ref[...] = jnp.zeros_like(c_ref)
    hf_ref[...] = jnp.zeros_like(hf_ref)

    def scan_body(s, ms):
        r0, hk, mf = chunk(s, 0, ms[0])
        hf_ref[pl.ds(r0, L), :] += hk
        r0, hk, mb = chunk(jnp.where(s < nc_ctx, nc_ctx - 1 - s, nc + nc_ctx - 1 - s), 1, ms[1])
        hf_ref[pl.ds(r0, L), :] += hk
        return (mf, mb)

    lax.fori_loop(0, nc, scan_body, (m0, m0))

    def finish_body(s, carry):
        r0 = pl.multiple_of(s * L, L)
        y = hf_ref[pl.ds(r0, L), :]
        y = y * lax.rsqrt(jnp.mean(y * y, axis=-1, keepdims=True) + NORM_EPS) * ng_ref[...]
        y = y * jax.nn.sigmoid(o_ref[0, pl.ds(r0, L), :].astype(f32))
        out_ref[0, pl.ds(r0, L), :] = y.astype(out_ref.dtype)
        return carry

    lax.fori_loop(0, nc, finish_body, 0)


def mlstm_mixer(qkv, o_src, o_off, gate_small, gate_b, norm_g, ctx):
    b, s, _ = qkv.shape
    nh = gate_b.shape[-1]
    vtot = norm_g.shape[-1]
    dv = vtot // nh
    dk = (qkv.shape[-1] - vtot) // (2 * nh)
    L = MLSTM_CHUNK
    nc, nc_ctx = s // L, ctx // L
    assert s % L == 0 and ctx % L == 0 and dk % LANE == 0 and dv % LANE == 0 and o_off % dv == 0
    assert 4 * nh <= LANE and gate_small.shape[-1] == LANE
    gt = chunk_transpose(gate_small, L)
    gb = jnp.pad(gate_b.astype(f32).reshape(4, nh).T, ((0, 0), (0, 4)))
    gb = jnp.broadcast_to(gb[:, :, None], (nh, 8, L))
    gb_c = jnp.pad(gate_b.astype(f32).reshape(1, 4 * nh), ((0, 0), (0, LANE - 4 * nh)))
    ng = norm_g.astype(f32)[None, :]
    koff = nh
    voff = 2 * nh * dk // dv
    ooff = o_off // dv
    return pl.pallas_call(
        functools.partial(_mlstm_kernel, nc_ctx=nc_ctx, nc=nc, dk=dk, dv=dv, n_heads=nh),
        grid=(b, nh),
        in_specs=[
            pl.BlockSpec((1, s, dk), lambda bi, hi: (bi, 0, hi)),
            pl.BlockSpec((1, s, dk), lambda bi, hi: (bi, 0, koff + hi)),
            pl.BlockSpec((1, s, dv), lambda bi, hi: (bi, 0, voff + hi)),
            pl.BlockSpec((1, s, dv), lambda bi, hi: (bi, 0, ooff + hi)),
            pl.BlockSpec((1, nc, LANE, L), lambda bi, hi: (bi, 0, 0, 0)),
            pl.BlockSpec((1, 8, L), lambda bi, hi: (hi, 0, 0)),
            pl.BlockSpec((1, s, LANE), lambda bi, hi: (bi, 0, 0)),
            pl.BlockSpec((1, LANE), lambda bi, hi: (0, 0)),
            pl.BlockSpec((1, dv), lambda bi, hi: (0, hi)),
        ],
        out_specs=pl.BlockSpec((1, s, dv), lambda bi, hi: (bi, 0, hi)),
        out_shape=jax.ShapeDtypeStruct((b, s, vtot), bf16),
        scratch_shapes=[pltpu.VMEM((s, dv), f32), pltpu.VMEM((2, dk, dv + LANE), f32)],
        compiler_params=_cparams(("parallel", "parallel")),
        name="mlstm_mixer",
    )(qkv, qkv, qkv, o_src, gt, gb, gate_small, gb_c, ng)


def _outproj_kernel(y_ref, w_ref, xl_ref, xc_ref, g_ref, ol_ref, oc_ref, *, n_ctx, cols, d):
    j = pl.program_id(1)
    acc = jnp.dot(y_ref[0], w_ref[...], preferred_element_type=f32)

    @pl.when(j < n_ctx)
    def _():
        oc_ref[0] = xc_ref[0] + g_ref[0, 0:1, :] * acc

    @pl.when(j >= n_ctx)
    def _():
        gl = g_ref[0, 1:2, :]
        if cols == 0:
            ol_ref[0] = xl_ref[0] + gl * acc
        else:
            nr = xl_ref.shape[1]
            for ci in range(cols):
                ol_ref[0, :, ci * d:(ci + 1) * d] = xl_ref[0, :, ci * d:(ci + 1) * d] + gl * acc[ci * nr:(ci + 1) * nr, :]


def outproj_residual(y, w, x_lat, x_ctx, gates, col_major):
    b, t, d = x_lat.shape
    ctx = x_ctx.shape[1]
    kin = y.shape[-1]
    tt = ROW_TILE
    n_ctx, n_lat = ctx // tt, t // tt
    if col_major:
        xl, xl_spec, cols = _col_view(x_lat, tt, n_ctx)
    else:
        cols = 0
        xl = x_lat
        xl_spec = pl.BlockSpec((1, tt, d), lambda bi, j: (bi, jnp.maximum(j - n_ctx, 0), 0))
    ol, oc = pl.pallas_call(
        functools.partial(_outproj_kernel, n_ctx=n_ctx, cols=cols, d=d),
        grid=(b, n_ctx + n_lat),
        in_specs=[
            pl.BlockSpec((1, tt, kin), lambda bi, j: (bi, j, 0)),
            pl.BlockSpec((kin, d), lambda bi, j: (0, 0), pipeline_mode=pl.Buffered(1)),
            xl_spec,
            pl.BlockSpec((1, tt, d), lambda bi, j: (bi, jnp.minimum(j, n_ctx - 1), 0)),
            pl.BlockSpec((1, 2, d), lambda bi, j: (bi, 0, 0)),
        ],
        out_specs=[xl_spec, pl.BlockSpec((1, tt, d), lambda bi, j: (bi, jnp.minimum(j, n_ctx - 1), 0))],
        out_shape=[jax.ShapeDtypeStruct(xl.shape, f32), jax.ShapeDtypeStruct(x_ctx.shape, f32)],
        compiler_params=_cparams(("parallel", "arbitrary")),
        name="outproj_residual",
    )(y, w, xl, x_ctx, gates)
    return ol.reshape(b, t, d), oc


def _mod_kernel(c_ref, w_ref, b_ref, o_ref):
    cvec = c_ref[...]
    cs = (cvec * jax.nn.sigmoid(cvec)).astype(bf16)
    o_ref[0] = jnp.dot(cs, w_ref[0].astype(bf16), preferred_element_type=f32) + b_ref[0]


def modulation(c_rows, mod_w, mod_b):
    depth, d, n = mod_w.shape
    tn = 1024
    assert n % tn == 0
    return pl.pallas_call(
        _mod_kernel,
        grid=(depth, n // tn),
        in_specs=[
            pl.BlockSpec(c_rows.shape, lambda i, j: (0, 0)),
            pl.BlockSpec((1, d, tn), lambda i, j: (i, 0, j)),
            pl.BlockSpec((1, 1, tn), lambda i, j: (i, 0, j)),
        ],
        out_specs=pl.BlockSpec((1, c_rows.shape[0], tn), lambda i, j: (i, 0, j)),
        out_shape=jax.ShapeDtypeStruct((depth, c_rows.shape[0], n), f32),
        compiler_params=_cparams(("parallel", "parallel")),
        name="modulation",
    )(c_rows, mod_w, mod_b.reshape(depth, 1, n))


def _moe_kernel(be_ref, nu_ref, x_ref, wgu_ref, bgu_ref, wd_ref, bd_ref, o_ref, *, f):
    i = pl.program_id(0)
    dh = x_ref.shape[1]

    @pl.when(i < nu_ref[0])
    def _():
        xa, xb = _unpack_pair(x_ref[...])
        h = (jnp.dot(xa.astype(bf16), wgu_ref[0, 0, :dh, :].astype(bf16), preferred_element_type=f32)
             + jnp.dot(xb.astype(bf16), wgu_ref[0, 0, dh:, :].astype(bf16), preferred_element_type=f32) + bgu_ref[0, 0])
        glu = jnp.minimum(h[:, :f], SWIGLU_LIMIT)
        lin = jnp.clip(h[:, f:], -SWIGLU_LIMIT, SWIGLU_LIMIT)
        act = glu * jax.nn.sigmoid(SWIGLU_ALPHA * glu) * (lin + 1.0)
        y = jnp.dot(act.astype(bf16), wd_ref[0, 0].astype(bf16), preferred_element_type=f32) + bd_ref[0, 0]
        yb = y.astype(bf16).astype(f32)
        o_ref[...] = _pack_pair(yb[:, :dh], yb[:, dh:])

    @pl.when(i >= nu_ref[0])
    def _():
        o_ref[...] = jnp.zeros_like(o_ref)


def moe_experts(xs, block_e, n_used, layer, gu_w, gu_b, down_w, down_b, tm):
    n_slots, dh = xs.shape
    depth, e, d, f2 = gu_w.shape
    f = f2 // 2
    n_blocks = n_slots // tm
    last = lambda i, nu: jnp.minimum(i, nu[0] - 1)
    grid_spec = pltpu.PrefetchScalarGridSpec(
        num_scalar_prefetch=2,
        grid=(n_blocks,),
        in_specs=[
            pl.BlockSpec((tm, dh), lambda i, be, nu: (last(i, nu), 0)),
            pl.BlockSpec((1, 1, d, f2), lambda i, be, nu: (layer, be[last(i, nu)], 0, 0)),
            pl.BlockSpec((1, 1, 1, f2), lambda i, be, nu: (layer, be[last(i, nu)], 0, 0)),
            pl.BlockSpec((1, 1, f, d), lambda i, be, nu: (layer, be[last(i, nu)], 0, 0)),
            pl.BlockSpec((1, 1, 1, d), lambda i, be, nu: (layer, be[last(i, nu)], 0, 0)),
        ],
        out_specs=pl.BlockSpec((tm, dh), lambda i, be, nu: (i, 0)),
    )
    return pl.pallas_call(
        functools.partial(_moe_kernel, f=f),
        grid_spec=grid_spec,
        out_shape=jax.ShapeDtypeStruct((n_slots, dh), jnp.uint32),
        compiler_params=_cparams(("arbitrary",)),
        name="moe_experts",
    )(block_e, n_used, xs, gu_w, gu_b.reshape(depth, e, 1, f2), down_w, down_b.reshape(depth, e, 1, d))


ROUTE_TILE = 512


def _route_kernel(lg_ref, rb_ref, ri_ref, rg_ref, cnt_ref, run_ref, *, n_exp):
    i = pl.program_id(0)
    tb = lg_ref.shape[0]
    i32 = jnp.int32

    @pl.when(i == 0)
    def _():
        run_ref[...] = jnp.zeros_like(run_ref)

    lane = lax.broadcasted_iota(i32, (tb, LANE), 1)
    lg = jnp.where(lane < n_exp, lg_ref[...] + rb_ref[...], -jnp.inf)
    vals, idxs, hots = [], [], []
    for _ in range(TOP_K):
        m = jnp.max(lg, axis=1, keepdims=True)
        idx = jnp.min(jnp.where(lg == m, lane, LANE), axis=1, keepdims=True)
        hot = lane == idx
        lg = jnp.where(hot, -jnp.inf, lg)
        vals.append(m)
        idxs.append(idx)
        hots.append(hot)
    es = [jnp.exp(v - vals[0]) for v in vals]
    tot = es[0]
    for ev in es[1:]:
        tot = tot + ev
    msum = hots[0].astype(f32)
    for hot in hots[1:]:
        msum = msum + hot.astype(f32)
    tri = (lax.broadcasted_iota(i32, (tb, tb), 0) > lax.broadcasted_iota(i32, (tb, tb), 1)).astype(bf16)
    before = jnp.dot(tri, msum.astype(bf16), preferred_element_type=f32) + run_ref[...]
    ri = jnp.zeros((tb, LANE), i32)
    rg = jnp.zeros((tb, LANE), f32)
    for k in range(TOP_K):
        rank = jnp.sum(jnp.where(hots[k], before, 0.0), axis=1, keepdims=True)
        ri = jnp.where(lane == k, idxs[k], ri)
        ri = jnp.where(lane == TOP_K + k, rank.astype(i32), ri)
        rg = jnp.where(lane == k, es[k] / tot, rg)
    ri_ref[...] = ri
    rg_ref[...] = rg
    run_ref[...] = run_ref[...] + jnp.sum(msum, axis=0, keepdims=True)
    cnt_ref[...] = run_ref[...]


def route(logits, router_b, n_exp):
    n = logits.shape[0]
    tb = _pick_tile(n, (ROUTE_TILE, 256))
    return pl.pallas_call(
        functools.partial(_route_kernel, n_exp=n_exp),
        grid=(n // tb,),
        in_specs=[pl.BlockSpec((tb, LANE), lambda i: (i, 0)), pl.BlockSpec((1, LANE), lambda i: (0, 0))],
        out_specs=[pl.BlockSpec((tb, LANE), lambda i: (i, 0)), pl.BlockSpec((tb, LANE), lambda i: (i, 0)),
                   pl.BlockSpec((1, LANE), lambda i: (0, 0))],
        out_shape=[jax.ShapeDtypeStruct((n, LANE), jnp.int32), jax.ShapeDtypeStruct((n, LANE), f32),
                   jax.ShapeDtypeStruct((1, LANE), f32)],
        scratch_shapes=[pltpu.VMEM((1, LANE), f32)],
        compiler_params=_cparams(("arbitrary",)),
        name="route",
    )(logits, router_b)


def _row_copy(src, s_row, dst, d_row, sem):
    return pltpu.make_async_copy(src.at[pl.ds(s_row, 1)], dst.at[pl.ds(d_row, 1)], sem)


def _dispatch_kernel(zs_ref, zc_ref, nu_ref, slot_ref, u_ref, xs_ref, zero_ref, sem, *, n_exp, tm):
    i = pl.program_id(0)
    tb = u_ref.shape[0]
    n_blocks = xs_ref.shape[0] // tm

    @pl.when(i == 0)
    def _():
        zero_ref[...] = jnp.zeros_like(zero_ref)

        def block_copy(blk):
            return pltpu.make_async_copy(zero_ref, xs_ref.at[pl.ds(pl.multiple_of(blk * tm, tm), tm)], sem)

        def start_block(blk, carry):
            block_copy(blk).start()
            return carry

        def wait_block(blk, carry):
            block_copy(blk).wait()
            return carry

        lax.fori_loop(nu_ref[0], n_blocks, start_block, 0)
        lax.fori_loop(nu_ref[0], n_blocks, wait_block, 0)

        def per_expert(e, carry):
            def start(r, c):
                _row_copy(zero_ref, 0, xs_ref, zs_ref[e] + r, sem).start()
                return c

            def wait(r, c):
                _row_copy(zero_ref, 0, xs_ref, 0, sem).wait()
                return c

            lax.fori_loop(0, zc_ref[e], start, 0)
            lax.fori_loop(0, zc_ref[e], wait, 0)
            return carry

        lax.fori_loop(0, n_exp, per_expert, 0)

    def start(r, c):
        for k in range(TOP_K):
            _row_copy(u_ref, r, xs_ref, slot_ref[0, 0, r * TOP_K + k], sem).start(priority=k % 2)
        return c

    def wait(r, c):
        _row_copy(u_ref, 0, xs_ref, 0, sem).wait()
        return c

    lax.fori_loop(0, tb, start, 0, unroll=4)
    lax.fori_loop(0, tb * TOP_K, wait, 0, unroll=8)


def dispatch(u, slot, zero_start, zero_count, n_used, n_blocks, tm):
    n, w = u.shape
    tb = ROW_TILE
    n_exp = zero_start.shape[0]
    grid_spec = pltpu.PrefetchScalarGridSpec(
        num_scalar_prefetch=3,
        grid=(n // tb,),
        in_specs=[
            pl.BlockSpec((1, 1, tb * TOP_K), lambda i, zs, zc, nu: (i, 0, 0), memory_space=pltpu.SMEM),
            pl.BlockSpec((tb, w), lambda i, zs, zc, nu: (i, 0)),
        ],
        out_specs=pl.BlockSpec(memory_space=pl.ANY),
        scratch_shapes=[pltpu.VMEM((tm, w), jnp.uint32), pltpu.SemaphoreType.DMA],
    )
    return pl.pallas_call(
        functools.partial(_dispatch_kernel, n_exp=n_exp, tm=tm),
        grid_spec=grid_spec,
        out_shape=jax.ShapeDtypeStruct((n_blocks * tm, w), jnp.uint32),
        compiler_params=_cparams(("arbitrary",)),
        name="dispatch",
    )(zero_start, zero_count, n_used, slot.reshape(n // tb, 1, tb * TOP_K), u)


def _combine_kernel(slot_ref, rg_ref, xl_ref, xc_ref, g_ref, ys_ref, ol_ref, oc_ref, buf_ref, sem, *, n_ctx):
    j = pl.program_id(1)
    tb = rg_ref.shape[0]

    def start(r, c):
        for k in range(TOP_K):
            pltpu.make_async_copy(ys_ref.at[pl.ds(slot_ref[0, 0, r * TOP_K + k], 1)],
                                  buf_ref.at[k, pl.ds(r, 1)], sem).start(priority=k % 2)
        return c

    def wait(r, c):
        pltpu.make_async_copy(ys_ref.at[pl.ds(0, 1)], buf_ref.at[0, pl.ds(0, 1)], sem).wait()
        return c

    lax.fori_loop(0, tb, start, 0, unroll=4)
    lax.fori_loop(0, tb * TOP_K, wait, 0, unroll=8)

    gates = rg_ref[...]
    acc = None
    for k in range(TOP_K):
        ya, yb = _unpack_pair(buf_ref[k])
        term = gates[:, k:k + 1] * jnp.concatenate([ya, yb], axis=1)
        acc = term if acc is None else acc + term

    @pl.when(j < n_ctx)
    def _():
        oc_ref[0] = xc_ref[0] + g_ref[0, 0:1, :] * acc

    @pl.when(j >= n_ctx)
    def _():
        ol_ref[0] = xl_ref[0] + g_ref[0, 1:2, :] * acc


def combine(ys, slot, rg, x_lat, x_ctx, gates):
    b, t, d = x_lat.shape
    ctx = x_ctx.shape[1]
    tb = ROW_TILE
    n_ctx, n_lat = ctx // tb, t // tb
    steps = n_ctx + n_lat
    return pl.pallas_call(
        functools.partial(_combine_kernel, n_ctx=n_ctx),
        grid=(b, steps),
        in_specs=[
            pl.BlockSpec((1, 1, tb * TOP_K), lambda bi, j: (bi * steps + j, 0, 0), memory_space=pltpu.SMEM),
            pl.BlockSpec((tb, LANE), lambda bi, j: (bi * steps + j, 0)),
            pl.BlockSpec((1, tb, d), lambda bi, j: (bi, jnp.maximum(j - n_ctx, 0), 0)),
            pl.BlockSpec((1, tb, d), lambda bi, j: (bi, jnp.minimum(j, n_ctx - 1), 0)),
            pl.BlockSpec((1, 2, d), lambda bi, j: (bi, 0, 0)),
            pl.BlockSpec(memory_space=pl.ANY),
        ],
        out_specs=[pl.BlockSpec((1, tb, d), lambda bi, j: (bi, jnp.maximum(j - n_ctx, 0), 0)),
                   pl.BlockSpec((1, tb, d), lambda bi, j: (bi, jnp.minimum(j, n_ctx - 1), 0))],
        out_shape=[jax.ShapeDtypeStruct(x_lat.shape, f32), jax.ShapeDtypeStruct(x_ctx.shape, f32)],
        scratch_shapes=[pltpu.VMEM((TOP_K, tb, d // 2), jnp.uint32), pltpu.SemaphoreType.DMA],
        compiler_params=_cparams(("arbitrary", "arbitrary")),
        name="combine",
    )(slot.reshape(b * steps, 1, tb * TOP_K), rg, x_lat, x_ctx, gates, ys)


def _final_norm_kernel(x_ref, g_ref, o_ref):
    x = x_ref[0]
    o_ref[0] = (x * lax.rsqrt(jnp.mean(x * x, axis=-1, keepdims=True) + NORM_EPS)) * g_ref[...]


def final_norm(x, g):
    b, t, d = x.shape
    tt = 512 if t % 512 == 0 else ROW_TILE
    return pl.pallas_call(
        _final_norm_kernel,
        grid=(b, t // tt),
        in_specs=[pl.BlockSpec((1, tt, d), lambda bi, j: (bi, j, 0)), pl.BlockSpec((1, d), lambda bi, j: (0, 0))],
        out_specs=pl.BlockSpec((1, tt, d), lambda bi, j: (bi, j, 0)),
        out_shape=jax.ShapeDtypeStruct((b, t, d), f32),
        compiler_params=_cparams(("parallel", "parallel")),
        name="final_norm",
    )(x, g)


MOE_TILE = 512


def _pick_tile(n, prefs):
    for p in prefs:
        if n % p == 0:
            return p
    raise ValueError(f"no tile in {prefs} divides {n}")


def _pad_lanes(w):
    return jnp.pad(w, ((0, 0), (0, LANE - w.shape[1])))


def _project(h, w_main):
    b, s, d = h.shape
    m = b * s
    n = w_main.shape[1]
    out = matmul(h.reshape(m, d), w_main, bf16, _pick_tile(m, (1024, 512, 256)), _pick_tile(n, (1024, 512, 256, 128)))
    return out.reshape(b, s, n)


def _moe_ffn(u, logits, router_b, layer, gu_w, gu_b, down_w, down_b, x_lat, x_ctx, gates):
    n = u.shape[0]
    e = gu_w.shape[1]
    tm = MOE_TILE
    i32 = jnp.int32
    ri, rg, counts = route(logits, _pad_lanes(router_b.astype(f32)[None, :]), e)
    cnt = counts[0, :e].astype(i32)
    padded = (cnt + tm - 1) // tm * tm
    pad_end = jnp.cumsum(padded)
    pad_start = pad_end - padded
    n_blocks = -(-(n * TOP_K) // tm) + e
    top_idx, rank = ri[:, :TOP_K], ri[:, TOP_K:2 * TOP_K]
    hot = top_idx[:, :, None] == jnp.arange(e, dtype=i32)[None, None, :]
    slot = (jnp.sum(jnp.where(hot, pad_start[None, None, :], 0), axis=-1) + rank).reshape(-1)
    block_start = jnp.arange(n_blocks, dtype=i32) * tm
    block_e = jnp.minimum(jnp.sum((block_start[:, None] >= pad_end[None, :]).astype(i32), axis=1), e - 1)
    n_used = (pad_end[-1] // tm).astype(i32).reshape(1)
    xs = dispatch(u, slot, pad_start + cnt, padded - cnt, n_used, n_blocks, tm)
    ys = moe_experts(xs, block_e, n_used, layer, gu_w, gu_b, down_w, down_b, tm)
    return combine(ys, slot, rg, x_lat, x_ctx, gates)


def kernel(x, c, ctx, c_ctx, mod_w, mod_b, norm1_g, norm2_g, final_g, m_in_w, m_conv_w, m_conv_b, m_dt_bias, m_a_log, m_d, m_norm_g, m_out_w, l_in_w, l_conv_w, l_conv_b, l_gate_b, l_norm_g, l_out_w, router_w, router_b, e_gu_w, e_gu_b, e_down_w, e_down_b):
    b, t, d = x.shape
    n_ctx_tok = ctx.shape[1]
    depth = mod_w.shape[0]
    m_inner = m_out_w.shape[1]
    m_conv = m_conv_w.shape[-1]
    l_qkv = l_conv_w.shape[-1]
    l_vtot = l_out_w.shape[1]

    c_rows = jnp.zeros((16, d), f32).at[:b].set(c).at[b].set(c_ctx)
    mod_all = modulation(c_rows, mod_w, mod_b)

    xl, xc = x, ctx
    for i in range(depth):
        j = i // 2
        mod = mod_all[i].reshape(16, 6, d)
        lat, cx = mod[:b], jnp.broadcast_to(mod[b][None], (b, 6, d))

        def mods(k):
            return jnp.stack([cx[:, k], cx[:, k + 1], lat[:, k], lat[:, k + 1]], axis=1)

        def gate(k):
            return jnp.stack([cx[:, k], lat[:, k]], axis=1)

        if i % 2 == 0:
            w_in = m_in_w[j].astype(bf16)
            n_main = m_inner + m_conv
            h, dt_small = normmod(xl, xc, norm1_g[i][None], mods(0), _pad_lanes(w_in[:, n_main:]), False)
            proj = _project(h, w_in[:, :n_main])
            xbc = conv_silu(proj, m_conv_w[j], m_conv_b[j][None], m_inner, 512, n_ctx_tok)
            y = ssd_mixer(xbc, proj, dt_small, m_dt_bias[j], m_a_log[j], m_d[j], m_norm_g[j], n_ctx_tok)
            xl, xc = outproj_residual(y, m_out_w[j].astype(bf16), xl, xc, gate(2), False)
        else:
            w_in = l_in_w[j].astype(bf16)
            n_main = l_qkv + l_vtot
            h, gate_small = normmod(xl, xc, norm1_g[i][None], mods(0), _pad_lanes(w_in[:, n_main:]), True)
            proj = _project(h, w_in[:, :n_main])
            qkv = conv_silu(proj, l_conv_w[j], l_conv_b[j][None], 0, 512, n_ctx_tok)
            y = mlstm_mixer(qkv, proj, l_qkv, gate_small, l_gate_b[j], l_norm_g[j], n_ctx_tok)
            xl, xc = outproj_residual(y, l_out_w[j].astype(bf16), xl, xc, gate(2), True)

        u, logits = normmod(xl, xc, norm2_g[i][None], mods(3), _pad_lanes(router_w[i].astype(bf16)), False, pack=True)
        s = n_ctx_tok + t
        xl, xc = _moe_ffn(u.reshape(b * s, d // 2), logits.reshape(b * s, LANE), router_b[i], i, e_gu_w, e_gu_b,
                          e_down_w, e_down_b, xl, xc, gate(5))
    return final_norm(xl, final_g[None])
```

```python
import functools

import jax
import jax.numpy as jnp
import numpy as np
from jax import lax
from jax.experimental import pallas as pl
from jax.experimental.pallas import tpu as pltpu

f32 = jnp.float32
bf16 = jnp.bfloat16

NORM_EPS = 1e-6
CONV_W = 5
GRID_W = 64
M_D_STATE = 128
TOP_K = 4
SWIGLU_LIMIT = 7.0
SWIGLU_ALPHA = 1.702
SSD_CHUNK = 128
MLSTM_CHUNK = 256
ROW_TILE = 256
LANE = 128
V7X_VMEM_LIMIT = 60000 * 1024


def _cparams(sem):
    return pltpu.CompilerParams(dimension_semantics=sem, vmem_limit_bytes=V7X_VMEM_LIMIT)


def _pack_pair(a, b):
    u32 = jnp.uint32
    return (lax.bitcast_convert_type(a, u32) & u32(0xFFFF0000)) | (lax.bitcast_convert_type(b, u32) >> 16)


def _unpack_pair(w):
    u32 = jnp.uint32
    return (lax.bitcast_convert_type(w & u32(0xFFFF0000), f32), lax.bitcast_convert_type(w << 16, f32))


def _col_view(x_lat, tt, n_ctx):
    b, t, d = x_lat.shape
    r = t // GRID_W
    cols = tt // r
    assert tt % r == 0 and GRID_W % cols == 0
    spec = pl.BlockSpec((1, r, cols * d), lambda bi, j: (bi, 0, jnp.maximum(j - n_ctx, 0)))
    return x_lat.reshape(b, r, GRID_W * d), spec, cols


def _normmod_kernel(xl_ref, xc_ref, g_ref, mod_ref, ws_ref, h_ref, small_ref, *, n_ctx, cols, d, pack):
    j = pl.program_id(1)

    def rows(x, shift, scale, r0, nr):
        ms = jnp.mean(x * x, axis=-1, keepdims=True)
        y = (x * lax.rsqrt(ms + NORM_EPS)) * g_ref[...]
        hb = (y * (1.0 + scale) + shift).astype(bf16)
        if pack:
            hf = hb.astype(f32)
            h_ref[0, r0:r0 + nr, :] = _pack_pair(hf[:, :d // 2], hf[:, d // 2:])
        else:
            h_ref[0, r0:r0 + nr, :] = hb
        small_ref[0, r0:r0 + nr, :] = jnp.dot(hb, ws_ref[...], preferred_element_type=f32)

    @pl.when(j < n_ctx)
    def _():
        rows(xc_ref[0], mod_ref[0, 0:1, :], mod_ref[0, 1:2, :], 0, xc_ref.shape[1])

    @pl.when(j >= n_ctx)
    def _():
        if cols == 0:
            rows(xl_ref[0], mod_ref[0, 2:3, :], mod_ref[0, 3:4, :], 0, xl_ref.shape[1])
        else:
            nr = xl_ref.shape[1]
            for ci in range(cols):
                rows(xl_ref[0, :, ci * d:(ci + 1) * d], mod_ref[0, 2:3, :], mod_ref[0, 3:4, :], ci * nr, nr)


def normmod(x_lat, x_ctx, g, mods, w_small, col_major, pack=False):
    b, t, d = x_lat.shape
    ctx = x_ctx.shape[1]
    tt = ROW_TILE
    n_ctx, n_lat = ctx // tt, t // tt
    assert ctx % tt == 0 and t % tt == 0
    if col_major:
        xl, xl_spec, cols = _col_view(x_lat, tt, n_ctx)
    else:
        cols = 0
        xl = x_lat
        xl_spec = pl.BlockSpec((1, tt, d), lambda bi, j: (bi, jnp.maximum(j - n_ctx, 0), 0))
    s = ctx + t
    dh = d // 2 if pack else d
    return pl.pallas_call(
        functools.partial(_normmod_kernel, n_ctx=n_ctx, cols=cols, d=d, pack=pack),
        grid=(b, n_ctx + n_lat),
        in_specs=[
            xl_spec,
            pl.BlockSpec((1, tt, d), lambda bi, j: (bi, jnp.minimum(j, n_ctx - 1), 0)),
            pl.BlockSpec((1, d), lambda bi, j: (0, 0)),
            pl.BlockSpec((1, 4, d), lambda bi, j: (bi, 0, 0)),
            pl.BlockSpec((d, LANE), lambda bi, j: (0, 0)),
        ],
        out_specs=[
            pl.BlockSpec((1, tt, dh), lambda bi, j: (bi, j, 0)),
            pl.BlockSpec((1, tt, LANE), lambda bi, j: (bi, j, 0)),
        ],
        out_shape=[jax.ShapeDtypeStruct((b, s, dh), jnp.uint32 if pack else bf16),
                   jax.ShapeDtypeStruct((b, s, LANE), f32)],
        compiler_params=_cparams(("parallel", "arbitrary")),
        name="normmod",
    )(xl, x_ctx, g, mods, w_small)


def _mm_kernel(x_ref, w_ref, o_ref):
    o_ref[...] = jnp.dot(x_ref[...], w_ref[...], preferred_element_type=f32).astype(o_ref.dtype)


def matmul(x, w, out_dtype, tm, tn):
    m, k = x.shape
    n = w.shape[1]
    assert m % tm == 0 and n % tn == 0
    return pl.pallas_call(
        _mm_kernel,
        grid=(m // tm, n // tn),
        in_specs=[pl.BlockSpec((tm, k), lambda i, j: (i, 0)), pl.BlockSpec((k, tn), lambda i, j: (0, j))],
        out_specs=pl.BlockSpec((tm, tn), lambda i, j: (i, j)),
        out_shape=jax.ShapeDtypeStruct((m, n), out_dtype),
        compiler_params=_cparams(("parallel", "arbitrary")),
        name="matmul",
    )(x, w)


def _conv_kernel(u_ref, w_ref, b_ref, o_ref, pad_ref, *, ctx, tile):
    s, cb = u_ref.shape[1], u_ref.shape[2]
    half = CONV_W // 2
    zero8 = jnp.zeros((8, cb), f32)
    pad_ref[0:8, :] = zero8
    pad_ref[8 + ctx:16 + ctx, :] = zero8
    pad_ref[16 + s:24 + s, :] = zero8
    for r0 in range(0, s, tile):
        base = 8 if r0 < ctx else 16
        pad_ref[base + r0:base + r0 + tile, :] = u_ref[0, r0:r0 + tile, :].astype(f32)
    w = w_ref[...]
    bias = b_ref[...]
    for r0 in range(0, s, tile):
        base = 8 if r0 < ctx else 16
        big = pad_ref[base + r0 - 8:base + r0 + tile + 8, :]
        acc = bias + w[half:half + 1, :] * big[8:8 + tile, :]
        for k in range(CONV_W):
            if k != half:
                rolled = pltpu.roll(big, (half - k) % (tile + 16), axis=0)
                acc = acc + w[k:k + 1, :] * rolled[8:8 + tile, :]
        o_ref[0, r0:r0 + tile, :] = (acc * jax.nn.sigmoid(acc)).astype(o_ref.dtype)


def conv_silu(u, w, bias, lane_off, cb, ctx):
    b, s, _ = u.shape
    c = w.shape[1]
    assert c % cb == 0 and lane_off % cb == 0
    tile = ROW_TILE
    assert ctx % tile == 0 and s % tile == 0
    off = lane_off // cb
    return pl.pallas_call(
        functools.partial(_conv_kernel, ctx=ctx, tile=tile),
        grid=(b, c // cb),
        in_specs=[
            pl.BlockSpec((1, s, cb), lambda bi, j: (bi, 0, off + j)),
            pl.BlockSpec((CONV_W, cb), lambda bi, j: (0, j)),
            pl.BlockSpec((1, cb), lambda bi, j: (0, j)),
        ],
        out_specs=pl.BlockSpec((1, s, cb), lambda bi, j: (bi, 0, j)),
        out_shape=jax.ShapeDtypeStruct((b, s, c), bf16),
        scratch_shapes=[pltpu.VMEM((s + 24, cb), f32)],
        compiler_params=_cparams(("parallel", "parallel")),
        name="conv_silu",
    )(u, w, bias)


def _split3(a):
    hi = a.astype(bf16)
    r1 = a - hi.astype(f32)
    mid = r1.astype(bf16)
    lo = (r1 - mid.astype(f32)).astype(bf16)
    return jnp.concatenate([hi, mid, lo], axis=0)


def _softplus(x):
    return jnp.maximum(x, 0.0) + jnp.log1p(jnp.exp(-jnp.abs(x)))


def _dot_t(a, b):
    return lax.dot_general(a, b, (((0,), (0,)), ((), ())), preferred_element_type=f32)


def _dot_nt(a, b):
    return lax.dot_general(a, b, (((1,), (1,)), ((), ())), preferred_element_type=f32)


def _chunk_transpose_kernel(x_ref, o_ref):
    nc, _, chunk = o_ref.shape[1:]
    for ci in range(nc):
        o_ref[0, ci] = x_ref[0, ci * chunk:(ci + 1) * chunk, :].T


def chunk_transpose(x, chunk):
    b, s, w = x.shape
    nc = s // chunk
    return pl.pallas_call(
        _chunk_transpose_kernel,
        grid=(b,),
        in_specs=[pl.BlockSpec((1, s, w), lambda bi: (bi, 0, 0))],
        out_specs=pl.BlockSpec((1, nc, w, chunk), lambda bi: (bi, 0, 0, 0)),
        out_shape=jax.ShapeDtypeStruct((b, nc, w, chunk), f32),
        compiler_params=_cparams(("parallel",)),
        name="chunk_transpose",
    )(x)


def _hi_mid(v):
    hi = v.astype(bf16)
    return jnp.concatenate([hi, (v - hi.astype(f32)).astype(bf16)], axis=1)


def _ssd_kernel(x_ref, b_ref, c_ref, z_ref, dt_ref, bias_ref, a_ref, dtc_ref, biasc_ref, ac_ref, dskip_ref, ng_ref,
                o_ref, yf_ref, h_ref, earg_ref, echan_ref, *, nc_ctx, nc, heads, headdim, n_heads):
    L = SSD_CHUNK
    R, P = heads, headdim
    RP = R * P
    i32 = jnp.int32
    gi = pl.program_id(1)

    row = lax.broadcasted_iota(i32, (L, L), 0)
    col = lax.broadcasted_iota(i32, (L, L), 1)
    lane_lo = lax.broadcasted_iota(i32, (L, 2 * P), 1) < P
    tri_c = jnp.tile((col <= row).astype(bf16), (1, 2))

    for d in range(2):
        first = d * n_heads + gi * R
        er = lax.broadcasted_iota(i32, (2 * LANE, R * L), 0) % LANE
        ec = lax.broadcasted_iota(i32, (2 * LANE, R * L), 1) // L
        earg_ref[d] = (er == first + ec).astype(bf16)
        fr = lax.broadcasted_iota(i32, (2 * LANE, RP), 0) % LANE
        fc = lax.broadcasted_iota(i32, (2 * LANE, RP), 1) // P
        echan_ref[d] = (fr == first + fc).astype(bf16)

    def chunk(ci, d):
        fwd = d == 0
        tri = ((row <= col) if fwd else (row >= col)).astype(bf16)
        mask = (col <= row) if fwd else (col >= row)
        r0 = pl.multiple_of(ci * L, L)
        raw = dt_ref[0, ci, pl.ds(pl.multiple_of(d * n_heads + gi * R, 8), R), :]
        dt = _softplus(raw + bias_ref[0, d * R:(d + 1) * R, :])
        a = dt * a_ref[0, d * R:(d + 1) * R, :]
        cum3 = jnp.dot(_split3(a), tri, preferred_element_type=f32)
        cum = cum3[0:R] + cum3[R:2 * R] + cum3[2 * R:3 * R]
        srow = cum - jnp.log(jnp.maximum(dt, 1e-37))
        crow = jnp.concatenate([srow[h:h + 1, :] for h in range(R)], axis=1)
        dtc = _softplus(dtc_ref[0, pl.ds(r0, L), :] + biasc_ref[...])
        ac = dtc * ac_ref[...]
        ac_hi = ac.astype(bf16)
        ac_parts = jnp.concatenate([ac_hi, (ac - ac_hi.astype(f32)).astype(bf16)], axis=0)
        pre = jnp.dot(tri_c, ac_parts, preferred_element_type=f32)
        tot = pre[L - 1:L, :]
        cumc = pre if fwd else tot - pre + ac
        arg = jnp.dot(_hi_mid(cumc), earg_ref[d], preferred_element_type=f32) - crow
        ecum_x = jnp.dot(_hi_mid(jnp.exp(cumc)), echan_ref[d], preferred_element_type=f32)
        ws_x = jnp.dot(_hi_mid(jnp.exp(tot - cumc) * dtc), echan_ref[d], preferred_element_type=f32)
        etot_x = ecum_x[L - 1:L, :] if fwd else ecum_x[0:1, :]

        xc = x_ref[0, pl.ds(r0, L), :]
        bc = b_ref[0, pl.ds(r0, L), :]
        cc = c_ref[0, pl.ds(r0, L), :]
        cb = _dot_nt(cc, bc)
        ys = []
        for p in range(R // 2):
            w_pair = []
            for h in (2 * p, 2 * p + 1):
                dec = jnp.exp(jnp.where(mask, arg[:, h * L:(h + 1) * L], -jnp.inf))
                w_pair.append((dec * cb).astype(bf16))
            xp = xc[:, 2 * p * P:2 * (p + 1) * P]
            zero = jnp.zeros_like(xp)
            rhs = jnp.concatenate([jnp.where(lane_lo, xp, zero), jnp.where(lane_lo, zero, xp)], axis=0)
            ys.append(jnp.dot(jnp.concatenate(w_pair, axis=1), rhs, preferred_element_type=f32))
        y_intra = jnp.concatenate(ys, axis=1)
        ht = h_ref[d]
        y = y_intra + jnp.dot(cc, ht.astype(bf16), preferred_element_type=f32) * ecum_x
        xw = (xc.astype(f32) * ws_x).astype(bf16)
        h_ref[d] = ht * etot_x + _dot_t(bc, xw)
        return r0, y

    h_ref[...] = jnp.zeros_like(h_ref)
    yf_ref[...] = jnp.zeros_like(yf_ref)

    def scan_body(s, carry):
        r0, y = chunk(s, 0)
        yf_ref[pl.ds(r0, L), :] += y
        r0, y = chunk(jnp.where(s < nc_ctx, nc_ctx - 1 - s, nc + nc_ctx - 1 - s), 1)
        yf_ref[pl.ds(r0, L), :] += y
        return carry

    lax.fori_loop(0, nc, scan_body, 0, unroll=2)

    def finish_body(s, carry):
        r0 = pl.multiple_of(s * L, L)
        y = yf_ref[pl.ds(r0, L), :] + dskip_ref[...] * x_ref[0, pl.ds(r0, L), :].astype(f32)
        z = z_ref[0, pl.ds(r0, L), :].astype(f32)
        y = y * (z * jax.nn.sigmoid(z))
        y = y * lax.rsqrt(jnp.mean(y * y, axis=-1, keepdims=True) + NORM_EPS)
        o_ref[0, pl.ds(r0, L), :] = (y * ng_ref[...]).astype(o_ref.dtype)
        return carry

    lax.fori_loop(0, nc, finish_body, 0)


def ssd_mixer(xbc, z_src, dt_small, dt_bias, a_log, d_skip, norm_g, ctx):
    b, s, _ = xbc.shape
    n = M_D_STATE
    nh = a_log.shape[-1]
    inner = norm_g.shape[-1]
    p = inner // nh
    g = (xbc.shape[-1] - inner) // (2 * n)
    r = nh // g
    rp = r * p
    L = SSD_CHUNK
    nc, nc_ctx = s // L, ctx // L
    assert p * 2 == LANE and s % L == 0 and ctx % L == 0 and r % 8 == 0 and rp % LANE == 0
    dt_t = chunk_transpose(dt_small, L)

    def per_head(v):
        v = v.astype(f32).reshape(2, g, r).transpose(1, 0, 2).reshape(g, 2 * r, 1)
        return jnp.broadcast_to(v, (g, 2 * r, L))

    bias_t = per_head(dt_bias)
    neg_a = -jnp.exp(a_log.astype(f32))
    a_t = per_head(neg_a)
    assert 2 * nh <= LANE and dt_small.shape[-1] == LANE
    bias_c = jnp.pad(dt_bias.astype(f32).reshape(1, 2 * nh), ((0, 0), (0, LANE - 2 * nh)))
    a_c = jnp.pad(neg_a.reshape(1, 2 * nh), ((0, 0), (0, LANE - 2 * nh)))
    dskip = jnp.repeat(d_skip.astype(f32), p)[None, :]
    ng = norm_g.astype(f32)[None, :]
    boff = inner // n
    return pl.pallas_call(
        functools.partial(_ssd_kernel, nc_ctx=nc_ctx, nc=nc, heads=r, headdim=p, n_heads=nh),
        grid=(b, g),
        in_specs=[
            pl.BlockSpec((1, s, rp), lambda bi, gi: (bi, 0, gi)),
            pl.BlockSpec((1, s, n), lambda bi, gi: (bi, 0, boff + gi)),
            pl.BlockSpec((1, s, n), lambda bi, gi: (bi, 0, boff + g + gi)),
            pl.BlockSpec((1, s, rp), lambda bi, gi: (bi, 0, gi)),
            pl.BlockSpec((1, nc, LANE, L), lambda bi, gi: (bi, 0, 0, 0)),
            pl.BlockSpec((1, 2 * r, L), lambda bi, gi: (gi, 0, 0)),
            pl.BlockSpec((1, 2 * r, L), lambda bi, gi: (gi, 0, 0)),
            pl.BlockSpec((1, s, LANE), lambda bi, gi: (bi, 0, 0)),
            pl.BlockSpec((1, LANE), lambda bi, gi: (0, 0)),
            pl.BlockSpec((1, LANE), lambda bi, gi: (0, 0)),
            pl.BlockSpec((1, rp), lambda bi, gi: (0, gi)),
            pl.BlockSpec((1, rp), lambda bi, gi: (0, gi)),
        ],
        out_specs=pl.BlockSpec((1, s, rp), lambda bi, gi: (bi, 0, gi)),
        out_shape=jax.ShapeDtypeStruct((b, s, inner), bf16),
        scratch_shapes=[pltpu.VMEM((s, rp), f32), pltpu.VMEM((2, n, rp), f32),
                        pltpu.VMEM((2, 2 * LANE, r * L), bf16), pltpu.VMEM((2, 2 * LANE, rp), bf16)],
        compiler_params=_cparams(("parallel", "parallel")),
        name="ssd_mixer",
    )(xbc, xbc, xbc, z_src, dt_t, bias_t, a_t, dt_small, bias_c, a_c, dskip, ng)


def _stack_rows(rows, n):
    w = next(r for r in rows if r is not None).shape[1]
    rid = lax.broadcasted_iota(jnp.int32, (n, w), 0)
    out = jnp.zeros((n, w), f32)
    for i, r in enumerate(rows):
        if r is not None:
            out = jnp.where(rid == i, jnp.broadcast_to(r, (n, w)), out)
    return out


def _parts3(a):
    hi = a.astype(bf16).astype(f32)
    mid = (a - hi).astype(bf16).astype(f32)
    lo = a - hi - mid
    return [hi, mid, lo]


def _mlstm_kernel(q_ref, k_ref, v_ref, o_ref, g_ref, gb_ref, gc_ref, gbc_ref, ng_ref, out_ref, hf_ref, c_ref,
                  *, nc_ctx, nc, dk, dv, n_heads):
    L = MLSTM_CHUNK
    i32 = jnp.int32
    scale = dk ** -0.5
    log_scale = -0.5 * float(np.log(dk))
    hi_ = pl.program_id(1)
    row = lax.broadcasted_iota(i32, (L, L), 0)
    col = lax.broadcasted_iota(i32, (L, L), 1)
    lane_e = lax.broadcasted_iota(i32, (L, LANE), 1)
    tri_c = jnp.tile((col <= row).astype(bf16), (1, 3))

    def chunk(ci, d, m):
        fwd = d == 0
        tri = ((row <= col) if fwd else (row >= col)).astype(bf16)
        mask = (col <= row) if fwd else (col >= row)
        r0 = pl.multiple_of(ci * L, L)
        lane_i = (2 * d) * n_heads + hi_
        ip = g_ref[0, ci, pl.ds(lane_i, 1), :] + gb_ref[0, 2 * d:2 * d + 1, :]
        logf = -_softplus(-(g_ref[0, ci, pl.ds(lane_i + n_heads, 1), :] + gb_ref[0, 2 * d + 1:2 * d + 2, :]))
        cum3 = jnp.dot(_stack_rows(_parts3(logf), 16).astype(bf16), tri, preferred_element_type=f32)
        bb = cum3[0:1] + cum3[1:2] + cum3[2:3]
        btot = bb[:, L - 1:L] if fwd else bb[:, 0:1]
        u = ip - bb
        gcol = gc_ref[0, pl.ds(r0, L), :] + gbc_ref[...]
        lfc = -_softplus(-gcol)
        pre = jnp.dot(tri_c, _split3(lfc), preferred_element_type=f32)
        cumc = pre if fwd else pre[L - 1:L, :] - pre + lfc
        b_col = jnp.sum(jnp.where(lane_e == lane_i + n_heads, cumc, 0.0), axis=1, keepdims=True)
        u_col = jnp.sum(jnp.where(lane_e == lane_i, gcol, 0.0), axis=1, keepdims=True) - b_col
        um = jnp.where(mask, u, -jnp.inf)
        mu = jnp.maximum(m, jnp.max(um, axis=1, keepdims=True))
        m_out = b_col + mu
        q = q_ref[0, pl.ds(r0, L), :]
        k = k_ref[0, pl.ds(r0, L), :]
        v = v_ref[0, pl.ds(r0, L), :]
        w = jnp.exp(um - (mu - log_scale)) * _dot_nt(q, k)
        inter_w = jnp.exp(m - mu) * scale
        cx = c_ref[d]
        qc = jnp.dot(q, cx.astype(bf16), preferred_element_type=f32)
        num = jnp.dot(w.astype(bf16), v, preferred_element_type=f32) + inter_w * qc[:, :dv]
        den = jnp.sum(w, axis=1, keepdims=True) + inter_w * (qc[:, dv:dv + 1] + qc[:, dv + 1:dv + 2])
        hk = num / jnp.maximum(jnp.abs(den), jnp.exp(-m_out))
        m_new = jnp.maximum(btot + m, btot + jnp.max(u, axis=1, keepdims=True))
        e_col = jnp.exp(u_col + btot - m_new)
        e_hi = e_col.astype(bf16).astype(f32)
        e_ext = jnp.where(lane_e == 0, e_hi, jnp.where(lane_e == 1, e_col - e_hi, 0.0))
        vext = jnp.concatenate([v.astype(f32) * e_col, e_ext], axis=1).astype(bf16)
        c_ref[d] = jnp.exp(btot + m - m_new) * cx + _dot_t(k, vext)
        return r0, hk, m_new

    m0 = jnp.full((1, 1), -jnp.inf, f32)
    c_ref[...] = jnp.zeros_like(c_ref)
    hf_ref[...] = jnp.zeros_like(hf_ref)

    def scan_body(s, ms):
        r0, hk, mf = chunk(s, 0, ms[0])
        hf_ref[pl.ds(r0, L), :] += hk
        r0, hk, mb = chunk(jnp.where(s < nc_ctx, nc_ctx - 1 - s, nc + nc_ctx - 1 - s), 1, ms[1])
        hf_ref[pl.ds(r0, L), :] += hk
        return (mf, mb)

    lax.fori_loop(0, nc, scan_body, (m0, m0), unroll=2)

    def finish_body(s, carry):
        r0 = pl.multiple_of(s * L, L)
        y = hf_ref[pl.ds(r0, L), :]
        y = y * lax.rsqrt(jnp.mean(y * y, axis=-1, keepdims=True) + NORM_EPS) * ng_ref[...]
        y = y * jax.nn.sigmoid(o_ref[0, pl.ds(r0, L), :].astype(f32))
        out_ref[0, pl.ds(r0, L), :] = y.astype(out_ref.dtype)
        return carry

    lax.fori_loop(0, nc, finish_body, 0)


def mlstm_mixer(qkv, o_src, o_off, gate_small, gate_b, norm_g, ctx):
    b, s, _ = qkv.shape
    nh = gate_b.shape[-1]
    vtot = norm_g.shape[-1]
    dv = vtot // nh
    dk = (qkv.shape[-1] - vtot) // (2 * nh)
    L = MLSTM_CHUNK
    nc, nc_ctx = s // L, ctx // L
    assert s % L == 0 and ctx % L == 0 and dk % LANE == 0 and dv % LANE == 0 and o_off % dv == 0
    assert 4 * nh <= LANE and gate_small.shape[-1] == LANE
    gt = chunk_transpose(gate_small, L)
    gb = jnp.pad(gate_b.astype(f32).reshape(4, nh).T, ((0, 0), (0, 4)))
    gb = jnp.broadcast_to(gb[:, :, None], (nh, 8, L))
    gb_c = jnp.pad(gate_b.astype(f32).reshape(1, 4 * nh), ((0, 0), (0, LANE - 4 * nh)))
    ng = norm_g.astype(f32)[None, :]
    koff = nh
    voff = 2 * nh * dk // dv
    ooff = o_off // dv
    return pl.pallas_call(
        functools.partial(_mlstm_kernel, nc_ctx=nc_ctx, nc=nc, dk=dk, dv=dv, n_heads=nh),
        grid=(b, nh),
        in_specs=[
            pl.BlockSpec((1, s, dk), lambda bi, hi: (bi, 0, hi)),
            pl.BlockSpec((1, s, dk), lambda bi, hi: (bi, 0, koff + hi)),
            pl.BlockSpec((1, s, dv), lambda bi, hi: (bi, 0, voff + hi)),
            pl.BlockSpec((1, s, dv), lambda bi, hi: (bi, 0, ooff + hi)),
            pl.BlockSpec((1, nc, LANE, L), lambda bi, hi: (bi, 0, 0, 0)),
            pl.BlockSpec((1, 8, L), lambda bi, hi: (hi, 0, 0)),
            pl.BlockSpec((1, s, LANE), lambda bi, hi: (bi, 0, 0)),
            pl.BlockSpec((1, LANE), lambda bi, hi: (0, 0)),
            pl.BlockSpec((1, dv), lambda bi, hi: (0, hi)),
        ],
        out_specs=pl.BlockSpec((1, s, dv), lambda bi, hi: (bi, 0, hi)),
        out_shape=jax.ShapeDtypeStruct((b, s, vtot), bf16),
        scratch_shapes=[pltpu.VMEM((s, dv), f32), pltpu.VMEM((2, dk, dv + LANE), f32)],
        compiler_params=_cparams(("parallel", "parallel")),
        name="mlstm_mixer",
    )(qkv, qkv, qkv, o_src, gt, gb, gate_small, gb_c, ng)


def _outproj_kernel(y_ref, w_ref, xl_ref, xc_ref, g_ref, ol_ref, oc_ref, *, n_ctx, cols, d):
    j = pl.program_id(1)
    acc = jnp.dot(y_ref[0], w_ref[...], preferred_element_type=f32)

    @pl.when(j < n_ctx)
    def _():
        oc_ref[0] = xc_ref[0] + g_ref[0, 0:1, :] * acc

    @pl.when(j >= n_ctx)
    def _():
        gl = g_ref[0, 1:2, :]
        if cols == 0:
            ol_ref[0] = xl_ref[0] + gl * acc
        else:
            nr = xl_ref.shape[1]
            for ci in range(cols):
                ol_ref[0, :, ci * d:(ci + 1) * d] = xl_ref[0, :, ci * d:(ci + 1) * d] + gl * acc[ci * nr:(ci + 1) * nr, :]


def outproj_residual(y, w, x_lat, x_ctx, gates, col_major):
    b, t, d = x_lat.shape
    ctx = x_ctx.shape[1]
    kin = y.shape[-1]
    tt = ROW_TILE
    n_ctx, n_lat = ctx // tt, t // tt
    if col_major:
        xl, xl_spec, cols = _col_view(x_lat, tt, n_ctx)
    else:
        cols = 0
        xl = x_lat
        xl_spec = pl.BlockSpec((1, tt, d), lambda bi, j: (bi, jnp.maximum(j - n_ctx, 0), 0))
    ol, oc = pl.pallas_call(
        functools.partial(_outproj_kernel, n_ctx=n_ctx, cols=cols, d=d),
        grid=(b, n_ctx + n_lat),
        in_specs=[
            pl.BlockSpec((1, tt, kin), lambda bi, j: (bi, j, 0)),
            pl.BlockSpec((kin, d), lambda bi, j: (0, 0), pipeline_mode=pl.Buffered(1)),
            xl_spec,
            pl.BlockSpec((1, tt, d), lambda bi, j: (bi, jnp.minimum(j, n_ctx - 1), 0)),
            pl.BlockSpec((1, 2, d), lambda bi, j: (bi, 0, 0)),
        ],
        out_specs=[xl_spec, pl.BlockSpec((1, tt, d), lambda bi, j: (bi, jnp.minimum(j, n_ctx - 1), 0))],
        out_shape=[jax.ShapeDtypeStruct(xl.shape, f32), jax.ShapeDtypeStruct(x_ctx.shape, f32)],
        compiler_params=_cparams(("parallel", "arbitrary")),
        name="outproj_residual",
    )(y, w, xl, x_ctx, gates)
    return ol.reshape(b, t, d), oc


def _mod_kernel(c_ref, w_ref, b_ref, o_ref):
    cvec = c_ref[...]
    cs = (cvec * jax.nn.sigmoid(cvec)).astype(bf16)
    o_ref[0] = jnp.dot(cs, w_ref[0].astype(bf16), preferred_element_type=f32) + b_ref[0]


def modulation(c_rows, mod_w, mod_b):
    depth, d, n = mod_w.shape
    tn = 1024
    assert n % tn == 0
    return pl.pallas_call(
        _mod_kernel,
        grid=(depth, n // tn),
        in_specs=[
            pl.BlockSpec(c_rows.shape, lambda i, j: (0, 0)),
            pl.BlockSpec((1, d, tn), lambda i, j: (i, 0, j)),
            pl.BlockSpec((1, 1, tn), lambda i, j: (i, 0, j)),
        ],
        out_specs=pl.BlockSpec((1, c_rows.shape[0], tn), lambda i, j: (i, 0, j)),
        out_shape=jax.ShapeDtypeStruct((depth, c_rows.shape[0], n), f32),
        compiler_params=_cparams(("parallel", "parallel")),
        name="modulation",
    )(c_rows, mod_w, mod_b.reshape(depth, 1, n))


def _moe_kernel(be_ref, nu_ref, x_ref, wgu_ref, bgu_ref, wd_ref, bd_ref, o_ref, *, f):
    i = pl.program_id(0)
    dh = x_ref.shape[1]

    @pl.when(i < nu_ref[0])
    def _():
        xa, xb = _unpack_pair(x_ref[...])
        h = (jnp.dot(xa.astype(bf16), wgu_ref[0, 0, :dh, :].astype(bf16), preferred_element_type=f32)
             + jnp.dot(xb.astype(bf16), wgu_ref[0, 0, dh:, :].astype(bf16), preferred_element_type=f32) + bgu_ref[0, 0])
        glu = jnp.minimum(h[:, :f], SWIGLU_LIMIT)
        lin = jnp.clip(h[:, f:], -SWIGLU_LIMIT, SWIGLU_LIMIT)
        act = glu * jax.nn.sigmoid(SWIGLU_ALPHA * glu) * (lin + 1.0)
        y = jnp.dot(act.astype(bf16), wd_ref[0, 0].astype(bf16), preferred_element_type=f32) + bd_ref[0, 0]
        yb = y.astype(bf16).astype(f32)
        o_ref[...] = _pack_pair(yb[:, :dh], yb[:, dh:])

    @pl.when(i >= nu_ref[0])
    def _():
        o_ref[...] = jnp.zeros_like(o_ref)


def moe_experts(xs, block_e, n_used, layer, gu_w, gu_b, down_w, down_b, tm):
    n_slots, dh = xs.shape
    depth, e, d, f2 = gu_w.shape
    f = f2 // 2
    n_blocks = n_slots // tm
    last = lambda i, nu: jnp.minimum(i, nu[0] - 1)
    grid_spec = pltpu.PrefetchScalarGridSpec(
        num_scalar_prefetch=2,
        grid=(n_blocks,),
        in_specs=[
            pl.BlockSpec((tm, dh), lambda i, be, nu: (last(i, nu), 0)),
            pl.BlockSpec((1, 1, d, f2), lambda i, be, nu: (layer, be[last(i, nu)], 0, 0)),
            pl.BlockSpec((1, 1, 1, f2), lambda i, be, nu: (layer, be[last(i, nu)], 0, 0)),
            pl.BlockSpec((1, 1, f, d), lambda i, be, nu: (layer, be[last(i, nu)], 0, 0)),
            pl.BlockSpec((1, 1, 1, d), lambda i, be, nu: (layer, be[last(i, nu)], 0, 0)),
        ],
        out_specs=pl.BlockSpec((tm, dh), lambda i, be, nu: (i, 0)),
    )
    return pl.pallas_call(
        functools.partial(_moe_kernel, f=f),
        grid_spec=grid_spec,
        out_shape=jax.ShapeDtypeStruct((n_slots, dh), jnp.uint32),
        compiler_params=_cparams(("arbitrary",)),
        name="moe_experts",
    )(block_e, n_used, xs, gu_w, gu_b.reshape(depth, e, 1, f2), down_w, down_b.reshape(depth, e, 1, d))


ROUTE_TILE = 512


def _route_kernel(lg_ref, rb_ref, ri_ref, rg_ref, cnt_ref, run_ref, *, n_exp):
    i = pl.program_id(0)
    tb = lg_ref.shape[0]
    i32 = jnp.int32

    @pl.when(i == 0)
    def _():
        run_ref[...] = jnp.zeros_like(run_ref)

    lane = lax.broadcasted_iota(i32, (tb, LANE), 1)
    lg = jnp.where(lane < n_exp, lg_ref[...] + rb_ref[...], -jnp.inf)
    vals, idxs, hots = [], [], []
    for _ in range(TOP_K):
        m = jnp.max(lg, axis=1, keepdims=True)
        idx = jnp.min(jnp.where(lg == m, lane, LANE), axis=1, keepdims=True)
        hot = lane == idx
        lg = jnp.where(hot, -jnp.inf, lg)
        vals.append(m)
        idxs.append(idx)
        hots.append(hot)
    es = [jnp.exp(v - vals[0]) for v in vals]
    tot = es[0]
    for ev in es[1:]:
        tot = tot + ev
    msum = hots[0].astype(f32)
    for hot in hots[1:]:
        msum = msum + hot.astype(f32)
    tri = (lax.broadcasted_iota(i32, (tb, tb), 0) > lax.broadcasted_iota(i32, (tb, tb), 1)).astype(bf16)
    before = jnp.dot(tri, msum.astype(bf16), preferred_element_type=f32) + run_ref[...]
    ri = jnp.zeros((tb, LANE), i32)
    rg = jnp.zeros((tb, LANE), f32)
    for k in range(TOP_K):
        rank = jnp.sum(jnp.where(hots[k], before, 0.0), axis=1, keepdims=True)
        ri = jnp.where(lane == k, idxs[k], ri)
        ri = jnp.where(lane == TOP_K + k, rank.astype(i32), ri)
        rg = jnp.where(lane == k, es[k] / tot, rg)
    ri_ref[...] = ri
    rg_ref[...] = rg
    run_ref[...] = run_ref[...] + jnp.sum(msum, axis=0, keepdims=True)
    cnt_ref[...] = run_ref[...]


def route(logits, router_b, n_exp):
    n = logits.shape[0]
    tb = _pick_tile(n, (ROUTE_TILE, 256))
    return pl.pallas_call(
        functools.partial(_route_kernel, n_exp=n_exp),
        grid=(n // tb,),
        in_specs=[pl.BlockSpec((tb, LANE), lambda i: (i, 0)), pl.BlockSpec((1, LANE), lambda i: (0, 0))],
        out_specs=[pl.BlockSpec((tb, LANE), lambda i: (i, 0)), pl.BlockSpec((tb, LANE), lambda i: (i, 0)),
                   pl.BlockSpec((1, LANE), lambda i: (0, 0))],
        out_shape=[jax.ShapeDtypeStruct((n, LANE), jnp.int32), jax.ShapeDtypeStruct((n, LANE), f32),
                   jax.ShapeDtypeStruct((1, LANE), f32)],
        scratch_shapes=[pltpu.VMEM((1, LANE), f32)],
        compiler_params=_cparams(("arbitrary",)),
        name="route",
    )(logits, router_b)


def _row_copy(src, s_row, dst, d_row, sem):
    return pltpu.make_async_copy(src.at[pl.ds(s_row, 1)], dst.at[pl.ds(d_row, 1)], sem)


def _dispatch_kernel(zs_ref, zc_ref, nu_ref, slot_ref, u_ref, xs_ref, zero_ref, sem, *, n_exp, tm):
    i = pl.program_id(0)
    tb = u_ref.shape[0]
    n_blocks = xs_ref.shape[0] // tm

    @pl.when(i == 0)
    def _():
        zero_ref[...] = jnp.zeros_like(zero_ref)

        def block_copy(blk):
            return pltpu.make_async_copy(zero_ref, xs_ref.at[pl.ds(pl.multiple_of(blk * tm, tm), tm)], sem)

        def start_block(blk, carry):
            block_copy(blk).start()
            return carry

        def wait_block(blk, carry):
            block_copy(blk).wait()
            return carry

        lax.fori_loop(nu_ref[0], n_blocks, start_block, 0)
        lax.fori_loop(nu_ref[0], n_blocks, wait_block, 0)

        def per_expert(e, carry):
            def start(r, c):
                _row_copy(zero_ref, 0, xs_ref, zs_ref[e] + r, sem).start()
                return c

            def wait(r, c):
                _row_copy(zero_ref, 0, xs_ref, 0, sem).wait()
                return c

            lax.fori_loop(0, zc_ref[e], start, 0)
            lax.fori_loop(0, zc_ref[e], wait, 0)
            return carry

        lax.fori_loop(0, n_exp, per_expert, 0)

    def start(r, c):
        for k in range(TOP_K):
            _row_copy(u_ref, r, xs_ref, slot_ref[0, 0, r * TOP_K + k], sem).start(priority=k % 2)
        return c

    def wait(r, c):
        _row_copy(u_ref, 0, xs_ref, 0, sem).wait()
        return c

    lax.fori_loop(0, tb, start, 0, unroll=4)
    lax.fori_loop(0, tb * TOP_K, wait, 0, unroll=8)


def dispatch(u, slot, zero_start, zero_count, n_used, n_blocks, tm):
    n, w = u.shape
    tb = ROW_TILE
    n_exp = zero_start.shape[0]
    grid_spec = pltpu.PrefetchScalarGridSpec(
        num_scalar_prefetch=3,
        grid=(n // tb,),
        in_specs=[
            pl.BlockSpec((1, 1, tb * TOP_K), lambda i, zs, zc, nu: (i, 0, 0), memory_space=pltpu.SMEM),
            pl.BlockSpec((tb, w), lambda i, zs, zc, nu: (i, 0)),
        ],
        out_specs=pl.BlockSpec(memory_space=pl.ANY),
        scratch_shapes=[pltpu.VMEM((tm, w), jnp.uint32), pltpu.SemaphoreType.DMA],
    )
    return pl.pallas_call(
        functools.partial(_dispatch_kernel, n_exp=n_exp, tm=tm),
        grid_spec=grid_spec,
        out_shape=jax.ShapeDtypeStruct((n_blocks * tm, w), jnp.uint32),
        compiler_params=_cparams(("arbitrary",)),
        name="dispatch",
    )(zero_start, zero_count, n_used, slot.reshape(n // tb, 1, tb * TOP_K), u)


def _combine_kernel(slot_ref, rg_ref, xl_ref, xc_ref, g_ref, ys_ref, ol_ref, oc_ref, buf_ref, sem, *, n_ctx):
    j = pl.program_id(1)
    tb = rg_ref.shape[0]

    def start(r, c):
        for k in range(TOP_K):
            pltpu.make_async_copy(ys_ref.at[pl.ds(slot_ref[0, 0, r * TOP_K + k], 1)],
                                  buf_ref.at[k, pl.ds(r, 1)], sem).start(priority=k % 2)
        return c

    def wait(r, c):
        pltpu.make_async_copy(ys_ref.at[pl.ds(0, 1)], buf_ref.at[0, pl.ds(0, 1)], sem).wait()
        return c

    lax.fori_loop(0, tb, start, 0, unroll=4)
    lax.fori_loop(0, tb * TOP_K, wait, 0, unroll=8)

    gates = rg_ref[...]
    acc = None
    for k in range(TOP_K):
        ya, yb = _unpack_pair(buf_ref[k])
        term = gates[:, k:k + 1] * jnp.concatenate([ya, yb], axis=1)
        acc = term if acc is None else acc + term

    @pl.when(j < n_ctx)
    def _():
        oc_ref[0] = xc_ref[0] + g_ref[0, 0:1, :] * acc

    @pl.when(j >= n_ctx)
    def _():
        ol_ref[0] = xl_ref[0] + g_ref[0, 1:2, :] * acc


def combine(ys, slot, rg, x_lat, x_ctx, gates):
    b, t, d = x_lat.shape
    ctx = x_ctx.shape[1]
    tb = ROW_TILE
    n_ctx, n_lat = ctx // tb, t // tb
    steps = n_ctx + n_lat
    return pl.pallas_call(
        functools.partial(_combine_kernel, n_ctx=n_ctx),
        grid=(b, steps),
        in_specs=[
            pl.BlockSpec((1, 1, tb * TOP_K), lambda bi, j: (bi * steps + j, 0, 0), memory_space=pltpu.SMEM),
            pl.BlockSpec((tb, LANE), lambda bi, j: (bi * steps + j, 0)),
            pl.BlockSpec((1, tb, d), lambda bi, j: (bi, jnp.maximum(j - n_ctx, 0), 0)),
            pl.BlockSpec((1, tb, d), lambda bi, j: (bi, jnp.minimum(j, n_ctx - 1), 0)),
            pl.BlockSpec((1, 2, d), lambda bi, j: (bi, 0, 0)),
            pl.BlockSpec(memory_space=pl.ANY),
        ],
        out_specs=[pl.BlockSpec((1, tb, d), lambda bi, j: (bi, jnp.maximum(j - n_ctx, 0), 0)),
                   pl.BlockSpec((1, tb, d), lambda bi, j: (bi, jnp.minimum(j, n_ctx - 1), 0))],
        out_shape=[jax.ShapeDtypeStruct(x_lat.shape, f32), jax.ShapeDtypeStruct(x_ctx.shape, f32)],
        scratch_shapes=[pltpu.VMEM((TOP_K, tb, d // 2), jnp.uint32), pltpu.SemaphoreType.DMA],
        compiler_params=_cparams(("arbitrary", "arbitrary")),
        name="combine",
    )(slot.reshape(b * steps, 1, tb * TOP_K), rg, x_lat, x_ctx, gates, ys)


def _final_norm_kernel(x_ref, g_ref, o_ref):
    x = x_ref[0]
    o_ref[0] = (x * lax.rsqrt(jnp.mean(x * x, axis=-1, keepdims=True) + NORM_EPS)) * g_ref[...]


def final_norm(x, g):
    b, t, d = x.shape
    tt = 512 if t % 512 == 0 else ROW_TILE
    return pl.pallas_call(
        _final_norm_kernel,
        grid=(b, t // tt),
        in_specs=[pl.BlockSpec((1, tt, d), lambda bi, j: (bi, j, 0)), pl.BlockSpec((1, d), lambda bi, j: (0, 0))],
        out_specs=pl.BlockSpec((1, tt, d), lambda bi, j: (bi, j, 0)),
        out_shape=jax.ShapeDtypeStruct((b, t, d), f32),
        compiler_params=_cparams(("parallel", "parallel")),
        name="final_norm",
    )(x, g)


MOE_TILE = 512


def _pick_tile(n, prefs):
    for p in prefs:
        if n % p == 0:
            return p
    raise ValueError(f"no tile in {prefs} divides {n}")


def _pad_lanes(w):
    return jnp.pad(w, ((0, 0), (0, LANE - w.shape[1])))


def _project(h, w_main):
    b, s, d = h.shape
    m = b * s
    n = w_main.shape[1]
    out = matmul(h.reshape(m, d), w_main, bf16, _pick_tile(m, (1024, 512, 256)), _pick_tile(n, (1024, 512, 256, 128)))
    return out.reshape(b, s, n)


def _moe_ffn(u, logits, router_b, layer, gu_w, gu_b, down_w, down_b, x_lat, x_ctx, gates):
    n = u.shape[0]
    e = gu_w.shape[1]
    tm = MOE_TILE
    i32 = jnp.int32
    ri, rg, counts = route(logits, _pad_lanes(router_b.astype(f32)[None, :]), e)
    cnt = counts[0, :e].astype(i32)
    padded = (cnt + tm - 1) // tm * tm
    pad_end = jnp.cumsum(padded)
    pad_start = pad_end - padded
    n_blocks = -(-(n * TOP_K) // tm) + e
    top_idx, rank = ri[:, :TOP_K], ri[:, TOP_K:2 * TOP_K]
    hot = top_idx[:, :, None] == jnp.arange(e, dtype=i32)[None, None, :]
    slot = (jnp.sum(jnp.where(hot, pad_start[None, None, :], 0), axis=-1) + rank).reshape(-1)
    block_start = jnp.arange(n_blocks, dtype=i32) * tm
    block_e = jnp.minimum(jnp.sum((block_start[:, None] >= pad_end[None, :]).astype(i32), axis=1), e - 1)
    n_used = (pad_end[-1] // tm).astype(i32).reshape(1)
    xs = dispatch(u, slot, pad_start + cnt, padded - cnt, n_used, n_blocks, tm)
    ys = moe_experts(xs, block_e, n_used, layer, gu_w, gu_b, down_w, down_b, tm)
    return combine(ys, slot, rg, x_lat, x_ctx, gates)


def kernel(x, c, ctx, c_ctx, mod_w, mod_b, norm1_g, norm2_g, final_g, m_in_w, m_conv_w, m_conv_b, m_dt_bias, m_a_log, m_d, m_norm_g, m_out_w, l_in_w, l_conv_w, l_conv_b, l_gate_b, l_norm_g, l_out_w, router_w, router_b, e_gu_w, e_gu_b, e_down_w, e_down_b):
    b, t, d = x.shape
    n_ctx_tok = ctx.shape[1]
    depth = mod_w.shape[0]
    m_inner = m_out_w.shape[1]
    m_conv = m_conv_w.shape[-1]
    l_qkv = l_conv_w.shape[-1]
    l_vtot = l_out_w.shape[1]

    c_rows = jnp.zeros((16, d), f32).at[:b].set(c).at[b].set(c_ctx)
    mod_all = modulation(c_rows, mod_w, mod_b)

    xl, xc = x, ctx
    for i in range(depth):
        j = i // 2
        mod = mod_all[i].reshape(16, 6, d)
        lat, cx = mod[:b], jnp.broadcast_to(mod[b][None], (b, 6, d))

        def mods(k):
            return jnp.stack([cx[:, k], cx[:, k + 1], lat[:, k], lat[:, k + 1]], axis=1)

        def gate(k):
            return jnp.stack([cx[:, k], lat[:, k]], axis=1)

        if i % 2 == 0:
            w_in = m_in_w[j].astype(bf16)
            n_main = m_inner + m_conv
            h, dt_small = normmod(xl, xc, norm1_g[i][None], mods(0), _pad_lanes(w_in[:, n_main:]), False)
            proj = _project(h, w_in[:, :n_main])
            xbc = conv_silu(proj, m_conv_w[j], m_conv_b[j][None], m_inner, 512, n_ctx_tok)
            y = ssd_mixer(xbc, proj, dt_small, m_dt_bias[j], m_a_log[j], m_d[j], m_norm_g[j], n_ctx_tok)
            xl, xc = outproj_residual(y, m_out_w[j].astype(bf16), xl, xc, gate(2), False)
        else:
            w_in = l_in_w[j].astype(bf16)
            n_main = l_qkv + l_vtot
            h, gate_small = normmod(xl, xc, norm1_g[i][None], mods(0), _pad_lanes(w_in[:, n_main:]), True)
            proj = _project(h, w_in[:, :n_main])
            qkv = conv_silu(proj, l_conv_w[j], l_conv_b[j][None], 0, 512, n_ctx_tok)
            y = mlstm_mixer(qkv, proj, l_qkv, gate_small, l_gate_b[j], l_norm_g[j], n_ctx_tok)
            xl, xc = outproj_residual(y, l_out_w[j].astype(bf16), xl, xc, gate(2), True)

        u, logits = normmod(xl, xc, norm2_g[i][None], mods(3), _pad_lanes(router_w[i].astype(bf16)), False, pack=True)
        s = n_ctx_tok + t
        xl, xc = _moe_ffn(u.reshape(b * s, d // 2), logits.reshape(b * s, LANE), router_b[i], i, e_gu_w, e_gu_b,
                          e_down_w, e_down_b, xl, xc, gate(5))
    return final_norm(xl, final_g[None])
```

```python
import functools

import jax
import jax.numpy as jnp
import numpy as np
from jax import lax
from jax.experimental import pallas as pl
from jax.experimental.pallas import tpu as pltpu

f32 = jnp.float32
bf16 = jnp.bfloat16

NORM_EPS = 1e-6
CONV_W = 5
GRID_W = 64
M_D_STATE = 128
TOP_K = 4
SWIGLU_LIMIT = 7.0
SWIGLU_ALPHA = 1.702
SSD_CHUNK = 128
MLSTM_CHUNK = 256
ROW_TILE = 256
LANE = 128
V7X_VMEM_LIMIT = 60000 * 1024


def _cparams(sem):
    return pltpu.CompilerParams(dimension_semantics=sem, vmem_limit_bytes=V7X_VMEM_LIMIT)


def _pack_pair(a, b):
    u32 = jnp.uint32
    return (lax.bitcast_convert_type(a, u32) & u32(0xFFFF0000)) | (lax.bitcast_convert_type(b, u32) >> 16)


def _unpack_pair(w):
    u32 = jnp.uint32
    return (lax.bitcast_convert_type(w & u32(0xFFFF0000), f32), lax.bitcast_convert_type(w << 16, f32))


def _col_view(x_lat, tt, n_ctx):
    b, t, d = x_lat.shape
    r = t // GRID_W
    cols = tt // r
    assert tt % r == 0 and GRID_W % cols == 0
    spec = pl.BlockSpec((1, r, cols * d), lambda bi, j: (bi, 0, jnp.maximum(j - n_ctx, 0)))
    return x_lat.reshape(b, r, GRID_W * d), spec, cols


def _normmod_kernel(xl_ref, xc_ref, g_ref, mod_ref, ws_ref, h_ref, small_ref, *, n_ctx, cols, d, pack):
    j = pl.program_id(1)

    def rows(x, shift, scale, r0, nr):
        ms = jnp.mean(x * x, axis=-1, keepdims=True)
        y = (x * lax.rsqrt(ms + NORM_EPS)) * g_ref[...]
        hb = (y * (1.0 + scale) + shift).astype(bf16)
        if pack:
            hf = hb.astype(f32)
            h_ref[0, r0:r0 + nr, :] = _pack_pair(hf[:, :d // 2], hf[:, d // 2:])
        else:
            h_ref[0, r0:r0 + nr, :] = hb
        small_ref[0, r0:r0 + nr, :] = jnp.dot(hb, ws_ref[...], preferred_element_type=f32)

    @pl.when(j < n_ctx)
    def _():
        rows(xc_ref[0], mod_ref[0, 0:1, :], mod_ref[0, 1:2, :], 0, xc_ref.shape[1])

    @pl.when(j >= n_ctx)
    def _():
        if cols == 0:
            rows(xl_ref[0], mod_ref[0, 2:3, :], mod_ref[0, 3:4, :], 0, xl_ref.shape[1])
        else:
            nr = xl_ref.shape[1]
            for ci in range(cols):
                rows(xl_ref[0, :, ci * d:(ci + 1) * d], mod_ref[0, 2:3, :], mod_ref[0, 3:4, :], ci * nr, nr)


def normmod(x_lat, x_ctx, g, mods, w_small, col_major, pack=False):
    b, t, d = x_lat.shape
    ctx = x_ctx.shape[1]
    tt = ROW_TILE
    n_ctx, n_lat = ctx // tt, t // tt
    assert ctx % tt == 0 and t % tt == 0
    if col_major:
        xl, xl_spec, cols = _col_view(x_lat, tt, n_ctx)
    else:
        cols = 0
        xl = x_lat
        xl_spec = pl.BlockSpec((1, tt, d), lambda bi, j: (bi, jnp.maximum(j - n_ctx, 0), 0))
    s = ctx + t
    dh = d // 2 if pack else d
    return pl.pallas_call(
        functools.partial(_normmod_kernel, n_ctx=n_ctx, cols=cols, d=d, pack=pack),
        grid=(b, n_ctx + n_lat),
        in_specs=[
            xl_spec,
            pl.BlockSpec((1, tt, d), lambda bi, j: (bi, jnp.minimum(j, n_ctx - 1), 0)),
            pl.BlockSpec((1, d), lambda bi, j: (0, 0)),
            pl.BlockSpec((1, 4, d), lambda bi, j: (bi, 0, 0)),
            pl.BlockSpec((d, LANE), lambda bi, j: (0, 0)),
        ],
        out_specs=[
            pl.BlockSpec((1, tt, dh), lambda bi, j: (bi, j, 0)),
            pl.BlockSpec((1, tt, LANE), lambda bi, j: (bi, j, 0)),
        ],
        out_shape=[jax.ShapeDtypeStruct((b, s, dh), jnp.uint32 if pack else bf16),
                   jax.ShapeDtypeStruct((b, s, LANE), f32)],
        compiler_params=_cparams(("parallel", "arbitrary")),
        name="normmod",
    )(xl, x_ctx, g, mods, w_small)


def _mm_kernel(x_ref, w_ref, o_ref):
    o_ref[...] = jnp.dot(x_ref[...], w_ref[...], preferred_element_type=f32).astype(o_ref.dtype)


def matmul(x, w, out_dtype, tm, tn):
    m, k = x.shape
    n = w.shape[1]
    assert m % tm == 0 and n % tn == 0
    return pl.pallas_call(
        _mm_kernel,
        grid=(m // tm, n // tn),
        in_specs=[pl.BlockSpec((tm, k), lambda i, j: (i, 0)), pl.BlockSpec((k, tn), lambda i, j: (0, j))],
        out_specs=pl.BlockSpec((tm, tn), lambda i, j: (i, j)),
        out_shape=jax.ShapeDtypeStruct((m, n), out_dtype),
        compiler_params=_cparams(("parallel", "arbitrary")),
        name="matmul",
    )(x, w)


def _conv_kernel(u_ref, w_ref, b_ref, o_ref, pad_ref, *, ctx, tile):
    s, cb = u_ref.shape[1], u_ref.shape[2]
    half = CONV_W // 2
    zero8 = jnp.zeros((8, cb), f32)
    pad_ref[0:8, :] = zero8
    pad_ref[8 + ctx:16 + ctx, :] = zero8
    pad_ref[16 + s:24 + s, :] = zero8
    for r0 in range(0, s, tile):
        base = 8 if r0 < ctx else 16
        pad_ref[base + r0:base + r0 + tile, :] = u_ref[0, r0:r0 + tile, :].astype(f32)
    w = w_ref[...]
    bias = b_ref[...]
    for r0 in range(0, s, tile):
        base = 8 if r0 < ctx else 16
        big = pad_ref[base + r0 - 8:base + r0 + tile + 8, :]
        acc = bias + w[half:half + 1, :] * big[8:8 + tile, :]
        for k in range(CONV_W):
            if k != half:
                rolled = pltpu.roll(big, (half - k) % (tile + 16), axis=0)
                acc = acc + w[k:k + 1, :] * rolled[8:8 + tile, :]
        o_ref[0, r0:r0 + tile, :] = (acc * jax.nn.sigmoid(acc)).astype(o_ref.dtype)


def conv_silu(u, w, bias, lane_off, cb, ctx):
    b, s, _ = u.shape
    c = w.shape[1]
    assert c % cb == 0 and lane_off % cb == 0
    tile = ROW_TILE
    assert ctx % tile == 0 and s % tile == 0
    off = lane_off // cb
    return pl.pallas_call(
        functools.partial(_conv_kernel, ctx=ctx, tile=tile),
        grid=(b, c // cb),
        in_specs=[
            pl.BlockSpec((1, s, cb), lambda bi, j: (bi, 0, off + j)),
            pl.BlockSpec((CONV_W, cb), lambda bi, j: (0, j)),
            pl.BlockSpec((1, cb), lambda bi, j: (0, j)),
        ],
        out_specs=pl.BlockSpec((1, s, cb), lambda bi, j: (bi, 0, j)),
        out_shape=jax.ShapeDtypeStruct((b, s, c), bf16),
        scratch_shapes=[pltpu.VMEM((s + 24, cb), f32)],
        compiler_params=_cparams(("parallel", "parallel")),
        name="conv_silu",
    )(u, w, bias)


def _split3(a):
    hi = a.astype(bf16)
    r1 = a - hi.astype(f32)
    mid = r1.astype(bf16)
    lo = (r1 - mid.astype(f32)).astype(bf16)
    return jnp.concatenate([hi, mid, lo], axis=0)


def _softplus(x):
    return jnp.maximum(x, 0.0) + jnp.log1p(jnp.exp(-jnp.abs(x)))


def _dot_t(a, b):
    return lax.dot_general(a, b, (((0,), (0,)), ((), ())), preferred_element_type=f32)


def _dot_nt(a, b):
    return lax.dot_general(a, b, (((1,), (1,)), ((), ())), preferred_element_type=f32)


def _chunk_transpose_kernel(x_ref, o_ref):
    nc, _, chunk = o_ref.shape[1:]
    for ci in range(nc):
        o_ref[0, ci] = x_ref[0, ci * chunk:(ci + 1) * chunk, :].T


def chunk_transpose(x, chunk):
    b, s, w = x.shape
    nc = s // chunk
    return pl.pallas_call(
        _chunk_transpose_kernel,
        grid=(b,),
        in_specs=[pl.BlockSpec((1, s, w), lambda bi: (bi, 0, 0))],
        out_specs=pl.BlockSpec((1, nc, w, chunk), lambda bi: (bi, 0, 0, 0)),
        out_shape=jax.ShapeDtypeStruct((b, nc, w, chunk), f32),
        compiler_params=_cparams(("parallel",)),
        name="chunk_transpose",
    )(x)


def _hi_mid(v):
    hi = v.astype(bf16)
    return jnp.concatenate([hi, (v - hi.astype(f32)).astype(bf16)], axis=1)


def _ssd_kernel(x_ref, b_ref, c_ref, z_ref, dt_ref, bias_ref, a_ref, dtc_ref, biasc_ref, ac_ref, dskip_ref, ng_ref,
                o_ref, yf_ref, h_ref, earg_ref, echan_ref, *, nc_ctx, nc, heads, headdim, n_heads):
    L = SSD_CHUNK
    R, P = heads, headdim
    RP = R * P
    i32 = jnp.int32
    gi = pl.program_id(1)

    row = lax.broadcasted_iota(i32, (L, L), 0)
    col = lax.broadcasted_iota(i32, (L, L), 1)
    lane_lo = lax.broadcasted_iota(i32, (L, 2 * P), 1) < P
    tri_c = jnp.tile((col <= row).astype(bf16), (1, 2))

    for d in range(2):
        first = d * n_heads + gi * R
        er = lax.broadcasted_iota(i32, (2 * LANE, R * L), 0) % LANE
        ec = lax.broadcasted_iota(i32, (2 * LANE, R * L), 1) // L
        earg_ref[d] = (er == first + ec).astype(bf16)
        fr = lax.broadcasted_iota(i32, (2 * LANE, RP), 0) % LANE
        fc = lax.broadcasted_iota(i32, (2 * LANE, RP), 1) // P
        echan_ref[d] = (fr == first + fc).astype(bf16)

    def chunk(ci, d):
        fwd = d == 0
        tri = ((row <= col) if fwd else (row >= col)).astype(bf16)
        mask = (col <= row) if fwd else (col >= row)
        r0 = pl.multiple_of(ci * L, L)
        raw = dt_ref[0, ci, pl.ds(pl.multiple_of(d * n_heads + gi * R, 8), R), :]
        dt = _softplus(raw + bias_ref[0, d * R:(d + 1) * R, :])
        a = dt * a_ref[0, d * R:(d + 1) * R, :]
        cum3 = jnp.dot(_split3(a), tri, preferred_element_type=f32)
        cum = cum3[0:R] + cum3[R:2 * R] + cum3[2 * R:3 * R]
        srow = cum - jnp.log(jnp.maximum(dt, 1e-37))
        crow = jnp.concatenate([srow[h:h + 1, :] for h in range(R)], axis=1)
        dtc = _softplus(dtc_ref[0, pl.ds(r0, L), :] + biasc_ref[...])
        ac = dtc * ac_ref[...]
        ac_hi = ac.astype(bf16)
        ac_parts = jnp.concatenate([ac_hi, (ac - ac_hi.astype(f32)).astype(bf16)], axis=0)
        pre = jnp.dot(tri_c, ac_parts, preferred_element_type=f32)
        tot = pre[L - 1:L, :]
        cumc = pre if fwd else tot - pre + ac
        arg = jnp.dot(_hi_mid(cumc), earg_ref[d], preferred_element_type=f32) - crow
        ecum_x = jnp.dot(_hi_mid(jnp.exp(cumc)), echan_ref[d], preferred_element_type=f32)
        ws_x = jnp.dot(_hi_mid(jnp.exp(tot - cumc) * dtc), echan_ref[d], preferred_element_type=f32)
        etot_x = ecum_x[L - 1:L, :] if fwd else ecum_x[0:1, :]

        xc = x_ref[0, pl.ds(r0, L), :]
        bc = b_ref[0, pl.ds(r0, L), :]
        cc = c_ref[0, pl.ds(r0, L), :]
        cb = _dot_nt(cc, bc)
        ys = []
        for p in range(R // 2):
            w_pair = []
            for h in (2 * p, 2 * p + 1):
                dec = jnp.exp(jnp.where(mask, arg[:, h * L:(h + 1) * L], -jnp.inf))
                w_pair.append((dec * cb).astype(bf16))
            xp = xc[:, 2 * p * P:2 * (p + 1) * P]
            zero = jnp.zeros_like(xp)
            rhs = jnp.concatenate([jnp.where(lane_lo, xp, zero), jnp.where(lane_lo, zero, xp)], axis=0)
            ys.append(jnp.dot(jnp.concatenate(w_pair, axis=1), rhs, preferred_element_type=f32))
        y_intra = jnp.concatenate(ys, axis=1)
        ht = h_ref[d]
        y = y_intra + jnp.dot(cc, ht.astype(bf16), preferred_element_type=f32) * ecum_x
        xw = (xc.astype(f32) * ws_x).astype(bf16)
        h_ref[d] = ht * etot_x + _dot_t(bc, xw)
        return r0, y

    h_ref[...] = jnp.zeros_like(h_ref)
    yf_ref[...] = jnp.zeros_like(yf_ref)

    def scan_body(s, carry):
        r0, y = chunk(s, 0)
        yf_ref[pl.ds(r0, L), :] += y
        r0, y = chunk(jnp.where(s < nc_ctx, nc_ctx - 1 - s, nc + nc_ctx - 1 - s), 1)
        yf_ref[pl.ds(r0, L), :] += y
        return carry

    lax.fori_loop(0, nc, scan_body, 0, unroll=4)

    def finish_body(s, carry):
        r0 = pl.multiple_of(s * L, L)
        y = yf_ref[pl.ds(r0, L), :] + dskip_ref[...] * x_ref[0, pl.ds(r0, L), :].astype(f32)
        z = z_ref[0, pl.ds(r0, L), :].astype(f32)
        y = y * (z * jax.nn.sigmoid(z))
        y = y * lax.rsqrt(jnp.mean(y * y, axis=-1, keepdims=True) + NORM_EPS)
        o_ref[0, pl.ds(r0, L), :] = (y * ng_ref[...]).astype(o_ref.dtype)
        return carry

    lax.fori_loop(0, nc, finish_body, 0)


def ssd_mixer(xbc, z_src, dt_small, dt_bias, a_log, d_skip, norm_g, ctx):
    b, s, _ = xbc.shape
    n = M_D_STATE
    nh = a_log.shape[-1]
    inner = norm_g.shape[-1]
    p = inner // nh
    g = (xbc.shape[-1] - inner) // (2 * n)
    r = nh // g
    rp = r * p
    L = SSD_CHUNK
    nc, nc_ctx = s // L, ctx // L
    assert p * 2 == LANE and s % L == 0 and ctx % L == 0 and r % 8 == 0 and rp % LANE == 0
    dt_t = chunk_transpose(dt_small, L)

    def per_head(v):
        v = v.astype(f32).reshape(2, g, r).transpose(1, 0, 2).reshape(g, 2 * r, 1)
        return jnp.broadcast_to(v, (g, 2 * r, L))

    bias_t = per_head(dt_bias)
    neg_a = -jnp.exp(a_log.astype(f32))
    a_t = per_head(neg_a)
    assert 2 * nh <= LANE and dt_small.shape[-1] == LANE
    bias_c = jnp.pad(dt_bias.astype(f32).reshape(1, 2 * nh), ((0, 0), (0, LANE - 2 * nh)))
    a_c = jnp.pad(neg_a.reshape(1, 2 * nh), ((0, 0), (0, LANE - 2 * nh)))
    dskip = jnp.repeat(d_skip.astype(f32), p)[None, :]
    ng = norm_g.astype(f32)[None, :]
    boff = inner // n
    return pl.pallas_call(
        functools.partial(_ssd_kernel, nc_ctx=nc_ctx, nc=nc, heads=r, headdim=p, n_heads=nh),
        grid=(b, g),
        in_specs=[
            pl.BlockSpec((1, s, rp), lambda bi, gi: (bi, 0, gi)),
            pl.BlockSpec((1, s, n), lambda bi, gi: (bi, 0, boff + gi)),
            pl.BlockSpec((1, s, n), lambda bi, gi: (bi, 0, boff + g + gi)),
            pl.BlockSpec((1, s, rp), lambda bi, gi: (bi, 0, gi)),
            pl.BlockSpec((1, nc, LANE, L), lambda bi, gi: (bi, 0, 0, 0)),
            pl.BlockSpec((1, 2 * r, L), lambda bi, gi: (gi, 0, 0)),
            pl.BlockSpec((1, 2 * r, L), lambda bi, gi: (gi, 0, 0)),
            pl.BlockSpec((1, s, LANE), lambda bi, gi: (bi, 0, 0)),
            pl.BlockSpec((1, LANE), lambda bi, gi: (0, 0)),
            pl.BlockSpec((1, LANE), lambda bi, gi: (0, 0)),
            pl.BlockSpec((1, rp), lambda bi, gi: (0, gi)),
            pl.BlockSpec((1, rp), lambda bi, gi: (0, gi)),
        ],
        out_specs=pl.BlockSpec((1, s, rp), lambda bi, gi: (bi, 0, gi)),
        out_shape=jax.ShapeDtypeStruct((b, s, inner), bf16),
        scratch_shapes=[pltpu.VMEM((s, rp), f32), pltpu.VMEM((2, n, rp), f32),
                        pltpu.VMEM((2, 2 * LANE, r * L), bf16), pltpu.VMEM((2, 2 * LANE, rp), bf16)],
        compiler_params=_cparams(("parallel", "parallel")),
        name="ssd_mixer",
    )(xbc, xbc, xbc, z_src, dt_t, bias_t, a_t, dt_small, bias_c, a_c, dskip, ng)


def _stack_rows(rows, n):
    w = next(r for r in rows if r is not None).shape[1]
    rid = lax.broadcasted_iota(jnp.int32, (n, w), 0)
    out = jnp.zeros((n, w), f32)
    for i, r in enumerate(rows):
        if r is not None:
            out = jnp.where(rid == i, jnp.broadcast_to(r, (n, w)), out)
    return out


def _parts3(a):
    hi = a.astype(bf16).astype(f32)
    mid = (a - hi).astype(bf16).astype(f32)
    lo = a - hi - mid
    return [hi, mid, lo]


def _mlstm_kernel(q_ref, k_ref, v_ref, o_ref, g_ref, gb_ref, gc_ref, gbc_ref, ng_ref, out_ref, hf_ref, c_ref,
                  *, nc_ctx, nc, dk, dv, n_heads):
    L = MLSTM_CHUNK
    i32 = jnp.int32
    scale = dk ** -0.5
    log_scale = -0.5 * float(np.log(dk))
    hi_ = pl.program_id(1)
    row = lax.broadcasted_iota(i32, (L, L), 0)
    col = lax.broadcasted_iota(i32, (L, L), 1)
    lane_e = lax.broadcasted_iota(i32, (L, LANE), 1)
    tri_c = jnp.tile((col <= row).astype(bf16), (1, 3))

    def chunk(ci, d, m):
        fwd = d == 0
        tri = ((row <= col) if fwd else (row >= col)).astype(bf16)
        mask = (col <= row) if fwd else (col >= row)
        r0 = pl.multiple_of(ci * L, L)
        lane_i = (2 * d) * n_heads + hi_
        ip = g_ref[0, ci, pl.ds(lane_i, 1), :] + gb_ref[0, 2 * d:2 * d + 1, :]
        logf = -_softplus(-(g_ref[0, ci, pl.ds(lane_i + n_heads, 1), :] + gb_ref[0, 2 * d + 1:2 * d + 2, :]))
        cum3 = jnp.dot(_stack_rows(_parts3(logf), 16).astype(bf16), tri, preferred_element_type=f32)
        bb = cum3[0:1] + cum3[1:2] + cum3[2:3]
        btot = bb[:, L - 1:L] if fwd else bb[:, 0:1]
        u = ip - bb
        gcol = gc_ref[0, pl.ds(r0, L), :] + gbc_ref[...]
        lfc = -_softplus(-gcol)
        pre = jnp.dot(tri_c, _split3(lfc), preferred_element_type=f32)
        cumc = pre if fwd else pre[L - 1:L, :] - pre + lfc
        b_col = jnp.sum(jnp.where(lane_e == lane_i + n_heads, cumc, 0.0), axis=1, keepdims=True)
        u_col = jnp.sum(jnp.where(lane_e == lane_i, gcol, 0.0), axis=1, keepdims=True) - b_col
        um = jnp.where(mask, u, -jnp.inf)
        mu = jnp.maximum(m, jnp.max(um, axis=1, keepdims=True))
        m_out = b_col + mu
        q = q_ref[0, pl.ds(r0, L), :]
        k = k_ref[0, pl.ds(r0, L), :]
        v = v_ref[0, pl.ds(r0, L), :]
        w = jnp.exp(um - (mu - log_scale)) * _dot_nt(q, k)
        inter_w = jnp.exp(m - mu) * scale
        cx = c_ref[d]
        qc = jnp.dot(q, cx.astype(bf16), preferred_element_type=f32)
        num = jnp.dot(w.astype(bf16), v, preferred_element_type=f32) + inter_w * qc[:, :dv]
        den = jnp.sum(w, axis=1, keepdims=True) + inter_w * (qc[:, dv:dv + 1] + qc[:, dv + 1:dv + 2])
        hk = num / jnp.maximum(jnp.abs(den), jnp.exp(-m_out))
        m_new = jnp.maximum(btot + m, btot + jnp.max(u, axis=1, keepdims=True))
        e_col = jnp.exp(u_col + btot - m_new)
        e_hi = e_col.astype(bf16).astype(f32)
        e_ext = jnp.where(lane_e == 0, e_hi, jnp.where(lane_e == 1, e_col - e_hi, 0.0))
        vext = jnp.concatenate([v.astype(f32) * e_col, e_ext], axis=1).astype(bf16)
        c_ref[d] = jnp.exp(btot + m - m_new) * cx + _dot_t(k, vext)
        return r0, hk, m_new

    m0 = jnp.full((1, 1), -jnp.inf, f32)
    c_ref[...] = jnp.zeros_like(c_ref)
    hf_ref[...] = jnp.zeros_like(hf_ref)

    def scan_body(s, ms):
        r0, hk, mf = chunk(s, 0, ms[0])
        hf_ref[pl.ds(r0, L), :] += hk
        r0, hk, mb = chunk(jnp.where(s < nc_ctx, nc_ctx - 1 - s, nc + nc_ctx - 1 - s), 1, ms[1])
        hf_ref[pl.ds(r0, L), :] += hk
        return (mf, mb)

    lax.fori_loop(0, nc, scan_body, (m0, m0), unroll=2)

    def finish_body(s, carry):
        r0 = pl.multiple_of(s * L, L)
        y = hf_ref[pl.ds(r0, L), :]
        y = y * lax.rsqrt(jnp.mean(y * y, axis=-1, keepdims=True) + NORM_EPS) * ng_ref[...]
        y = y * jax.nn.sigmoid(o_ref[0, pl.ds(r0, L), :].astype(f32))
        out_ref[0, pl.ds(r0, L), :] = y.astype(out_ref.dtype)
        return carry

    lax.fori_loop(0, nc, finish_body, 0)


def mlstm_mixer(qkv, o_src, o_off, gate_small, gate_b, norm_g, ctx):
    b, s, _ = qkv.shape
    nh = gate_b.shape[-1]
    vtot = norm_g.shape[-1]
    dv = vtot // nh
    dk = (qkv.shape[-1] - vtot) // (2 * nh)
    L = MLSTM_CHUNK
    nc, nc_ctx = s // L, ctx // L
    assert s % L == 0 and ctx % L == 0 and dk % LANE == 0 and dv % LANE == 0 and o_off % dv == 0
    assert 4 * nh <= LANE and gate_small.shape[-1] == LANE
    gt = chunk_transpose(gate_small, L)
    gb = jnp.pad(gate_b.astype(f32).reshape(4, nh).T, ((0, 0), (0, 4)))
    gb = jnp.broadcast_to(gb[:, :, None], (nh, 8, L))
    gb_c = jnp.pad(gate_b.astype(f32).reshape(1, 4 * nh), ((0, 0), (0, LANE - 4 * nh)))
    ng = norm_g.astype(f32)[None, :]
    koff = nh
    voff = 2 * nh * dk // dv
    ooff = o_off // dv
    return pl.pallas_call(
        functools.partial(_mlstm_kernel, nc_ctx=nc_ctx, nc=nc, dk=dk, dv=dv, n_heads=nh),
        grid=(b, nh),
        in_specs=[
            pl.BlockSpec((1, s, dk), lambda bi, hi: (bi, 0, hi)),
            pl.BlockSpec((1, s, dk), lambda bi, hi: (bi, 0, koff + hi)),
            pl.BlockSpec((1, s, dv), lambda bi, hi: (bi, 0, voff + hi)),
            pl.BlockSpec((1, s, dv), lambda bi, hi: (bi, 0, ooff + hi)),
            pl.BlockSpec((1, nc, LANE, L), lambda bi, hi: (bi, 0, 0, 0)),
            pl.BlockSpec((1, 8, L), lambda bi, hi: (hi, 0, 0)),
            pl.BlockSpec((1, s, LANE), lambda bi, hi: (bi, 0, 0)),
            pl.BlockSpec((1, LANE), lambda bi, hi: (0, 0)),
            pl.BlockSpec((1, dv), lambda bi, hi: (0, hi)),
        ],
        out_specs=pl.BlockSpec((1, s, dv), lambda bi, hi: (bi, 0, hi)),
        out_shape=jax.ShapeDtypeStruct((b, s, vtot), bf16),
        scratch_shapes=[pltpu.VMEM((s, dv), f32), pltpu.VMEM((2, dk, dv + LANE), f32)],
        compiler_params=_cparams(("parallel", "parallel")),
        name="mlstm_mixer",
    )(qkv, qkv, qkv, o_src, gt, gb, gate_small, gb_c, ng)


def _outproj_kernel(y_ref, w_ref, xl_ref, xc_ref, g_ref, ol_ref, oc_ref, *, n_ctx, cols, d):
    j = pl.program_id(1)
    acc = jnp.dot(y_ref[0], w_ref[...], preferred_element_type=f32)

    @pl.when(j < n_ctx)
    def _():
        oc_ref[0] = xc_ref[0] + g_ref[0, 0:1, :] * acc

    @pl.when(j >= n_ctx)
    def _():
        gl = g_ref[0, 1:2, :]
        if cols == 0:
            ol_ref[0] = xl_ref[0] + gl * acc
        else:
            nr = xl_ref.shape[1]
            for ci in range(cols):
                ol_ref[0, :, ci * d:(ci + 1) * d] = xl_ref[0, :, ci * d:(ci + 1) * d] + gl * acc[ci * nr:(ci + 1) * nr, :]


def outproj_residual(y, w, x_lat, x_ctx, gates, col_major):
    b, t, d = x_lat.shape
    ctx = x_ctx.shape[1]
    kin = y.shape[-1]
    tt = ROW_TILE
    n_ctx, n_lat = ctx // tt, t // tt
    if col_major:
        xl, xl_spec, cols = _col_view(x_lat, tt, n_ctx)
    else:
        cols = 0
        xl = x_lat
        xl_spec = pl.BlockSpec((1, tt, d), lambda bi, j: (bi, jnp.maximum(j - n_ctx, 0), 0))
    ol, oc = pl.pallas_call(
        functools.partial(_outproj_kernel, n_ctx=n_ctx, cols=cols, d=d),
        grid=(b, n_ctx + n_lat),
        in_specs=[
            pl.BlockSpec((1, tt, kin), lambda bi, j: (bi, j, 0)),
            pl.BlockSpec((kin, d), lambda bi, j: (0, 0), pipeline_mode=pl.Buffered(1)),
            xl_spec,
            pl.BlockSpec((1, tt, d), lambda bi, j: (bi, jnp.minimum(j, n_ctx - 1), 0)),
            pl.BlockSpec((1, 2, d), lambda bi, j: (bi, 0, 0)),
        ],
        out_specs=[xl_spec, pl.BlockSpec((1, tt, d), lambda bi, j: (bi, jnp.minimum(j, n_ctx - 1), 0))],
        out_shape=[jax.ShapeDtypeStruct(xl.shape, f32), jax.ShapeDtypeStruct(x_ctx.shape, f32)],
        compiler_params=_cparams(("parallel", "arbitrary")),
        name="outproj_residual",
    )(y, w, xl, x_ctx, gates)
    return ol.reshape(b, t, d), oc


def _mod_kernel(c_ref, w_ref, b_ref, o_ref):
    cvec = c_ref[...]
    cs = (cvec * jax.nn.sigmoid(cvec)).astype(bf16)
    o_ref[0] = jnp.dot(cs, w_ref[0].astype(bf16), preferred_element_type=f32) + b_ref[0]


def modulation(c_rows, mod_w, mod_b):
    depth, d, n = mod_w.shape
    tn = 1024
    assert n % tn == 0
    return pl.pallas_call(
        _mod_kernel,
        grid=(depth, n // tn),
        in_specs=[
            pl.BlockSpec(c_rows.shape, lambda i, j: (0, 0)),
            pl.BlockSpec((1, d, tn), lambda i, j: (i, 0, j)),
            pl.BlockSpec((1, 1, tn), lambda i, j: (i, 0, j)),
        ],
        out_specs=pl.BlockSpec((1, c_rows.shape[0], tn), lambda i, j: (i, 0, j)),
        out_shape=jax.ShapeDtypeStruct((depth, c_rows.shape[0], n), f32),
        compiler_params=_cparams(("parallel", "parallel")),
        name="modulation",
    )(c_rows, mod_w, mod_b.reshape(depth, 1, n))


def _moe_kernel(be_ref, nu_ref, x_ref, wgu_ref, bgu_ref, wd_ref, bd_ref, o_ref, *, f):
    i = pl.program_id(0)
    dh = x_ref.shape[1]

    @pl.when(i < nu_ref[0])
    def _():
        xa, xb = _unpack_pair(x_ref[...])
        h = (jnp.dot(xa.astype(bf16), wgu_ref[0, 0, :dh, :].astype(bf16), preferred_element_type=f32)
             + jnp.dot(xb.astype(bf16), wgu_ref[0, 0, dh:, :].astype(bf16), preferred_element_type=f32) + bgu_ref[0, 0])
        glu = jnp.minimum(h[:, :f], SWIGLU_LIMIT)
        lin = jnp.clip(h[:, f:], -SWIGLU_LIMIT, SWIGLU_LIMIT)
        act = glu * jax.nn.sigmoid(SWIGLU_ALPHA * glu) * (lin + 1.0)
        y = jnp.dot(act.astype(bf16), wd_ref[0, 0].astype(bf16), preferred_element_type=f32) + bd_ref[0, 0]
        yb = y.astype(bf16).astype(f32)
        o_ref[...] = _pack_pair(yb[:, :dh], yb[:, dh:])

    @pl.when(i >= nu_ref[0])
    def _():
        o_ref[...] = jnp.zeros_like(o_ref)


def moe_experts(xs, block_e, n_used, layer, gu_w, gu_b, down_w, down_b, tm):
    n_slots, dh = xs.shape
    depth, e, d, f2 = gu_w.shape
    f = f2 // 2
    n_blocks = n_slots // tm
    last = lambda i, nu: jnp.minimum(i, nu[0] - 1)
    grid_spec = pltpu.PrefetchScalarGridSpec(
        num_scalar_prefetch=2,
        grid=(n_blocks,),
        in_specs=[
            pl.BlockSpec((tm, dh), lambda i, be, nu: (last(i, nu), 0)),
            pl.BlockSpec((1, 1, d, f2), lambda i, be, nu: (layer, be[last(i, nu)], 0, 0)),
            pl.BlockSpec((1, 1, 1, f2), lambda i, be, nu: (layer, be[last(i, nu)], 0, 0)),
            pl.BlockSpec((1, 1, f, d), lambda i, be, nu: (layer, be[last(i, nu)], 0, 0)),
            pl.BlockSpec((1, 1, 1, d), lambda i, be, nu: (layer, be[last(i, nu)], 0, 0)),
        ],
        out_specs=pl.BlockSpec((tm, dh), lambda i, be, nu: (i, 0)),
    )
    return pl.pallas_call(
        functools.partial(_moe_kernel, f=f),
        grid_spec=grid_spec,
        out_shape=jax.ShapeDtypeStruct((n_slots, dh), jnp.uint32),
        compiler_params=_cparams(("arbitrary",)),
        name="moe_experts",
    )(block_e, n_used, xs, gu_w, gu_b.reshape(depth, e, 1, f2), down_w, down_b.reshape(depth, e, 1, d))


ROUTE_TILE = 512


def _route_kernel(lg_ref, rb_ref, ri_ref, rg_ref, cnt_ref, run_ref, *, n_exp):
    i = pl.program_id(0)
    tb = lg_ref.shape[0]
    i32 = jnp.int32

    @pl.when(i == 0)
    def _():
        run_ref[...] = jnp.zeros_like(run_ref)

    lane = lax.broadcasted_iota(i32, (tb, LANE), 1)
    lg = jnp.where(lane < n_exp, lg_ref[...] + rb_ref[...], -jnp.inf)
    vals, idxs, hots = [], [], []
    for _ in range(TOP_K):
        m = jnp.max(lg, axis=1, keepdims=True)
        idx = jnp.min(jnp.where(lg == m, lane, LANE), axis=1, keepdims=True)
        hot = lane == idx
        lg = jnp.where(hot, -jnp.inf, lg)
        vals.append(m)
        idxs.append(idx)
        hots.append(hot)
    es = [jnp.exp(v - vals[0]) for v in vals]
    tot = es[0]
    for ev in es[1:]:
        tot = tot + ev
    msum = hots[0].astype(f32)
    for hot in hots[1:]:
        msum = msum + hot.astype(f32)
    tri = (lax.broadcasted_iota(i32, (tb, tb), 0) > lax.broadcasted_iota(i32, (tb, tb), 1)).astype(bf16)
    before = jnp.dot(tri, msum.astype(bf16), preferred_element_type=f32) + run_ref[...]
    ri = jnp.zeros((tb, LANE), i32)
    rg = jnp.zeros((tb, LANE), f32)
    for k in range(TOP_K):
        rank = jnp.sum(jnp.where(hots[k], before, 0.0), axis=1, keepdims=True)
        ri = jnp.where(lane == k, idxs[k], ri)
        ri = jnp.where(lane == TOP_K + k, rank.astype(i32), ri)
        rg = jnp.where(lane == k, es[k] / tot, rg)
    ri_ref[...] = ri
    rg_ref[...] = rg
    run_ref[...] = run_ref[...] + jnp.sum(msum, axis=0, keepdims=True)
    cnt_ref[...] = run_ref[...]


def route(logits, router_b, n_exp):
    n = logits.shape[0]
    tb = _pick_tile(n, (ROUTE_TILE, 256))
    return pl.pallas_call(
        functools.partial(_route_kernel, n_exp=n_exp),
        grid=(n // tb,),
        in_specs=[pl.BlockSpec((tb, LANE), lambda i: (i, 0)), pl.BlockSpec((1, LANE), lambda i: (0, 0))],
        out_specs=[pl.BlockSpec((tb, LANE), lambda i: (i, 0)), pl.BlockSpec((tb, LANE), lambda i: (i, 0)),
                   pl.BlockSpec((1, LANE), lambda i: (0, 0))],
        out_shape=[jax.ShapeDtypeStruct((n, LANE), jnp.int32), jax.ShapeDtypeStruct((n, LANE), f32),
                   jax.ShapeDtypeStruct((1, LANE), f32)],
        scratch_shapes=[pltpu.VMEM((1, LANE), f32)],
        compiler_params=_cparams(("arbitrary",)),
        name="route",
    )(logits, router_b)


def _row_copy(src, s_row, dst, d_row, sem):
    return pltpu.make_async_copy(src.at[pl.ds(s_row, 1)], dst.at[pl.ds(d_row, 1)], sem)


def _dispatch_kernel(zs_ref, zc_ref, nu_ref, slot_ref, u_ref, xs_ref, zero_ref, sem, *, n_exp, tm):
    i = pl.program_id(0)
    tb = u_ref.shape[0]
    n_blocks = xs_ref.shape[0] // tm

    @pl.when(i == 0)
    def _():
        zero_ref[...] = jnp.zeros_like(zero_ref)

        def block_copy(blk):
            return pltpu.make_async_copy(zero_ref, xs_ref.at[pl.ds(pl.multiple_of(blk * tm, tm), tm)], sem)

        def start_block(blk, carry):
            block_copy(blk).start()
            return carry

        def wait_block(blk, carry):
            block_copy(blk).wait()
            return carry

        lax.fori_loop(nu_ref[0], n_blocks, start_block, 0)
        lax.fori_loop(nu_ref[0], n_blocks, wait_block, 0)

        def per_expert(e, carry):
            def start(r, c):
                _row_copy(zero_ref, 0, xs_ref, zs_ref[e] + r, sem).start()
                return c

            def wait(r, c):
                _row_copy(zero_ref, 0, xs_ref, 0, sem).wait()
                return c

            lax.fori_loop(0, zc_ref[e], start, 0)
            lax.fori_loop(0, zc_ref[e], wait, 0)
            return carry

        lax.fori_loop(0, n_exp, per_expert, 0)

    def start(r, c):
        for k in range(TOP_K):
            _row_copy(u_ref, r, xs_ref, slot_ref[0, 0, r * TOP_K + k], sem).start(priority=k % 2)
        return c

    def wait(r, c):
        _row_copy(u_ref, 0, xs_ref, 0, sem).wait()
        return c

    lax.fori_loop(0, tb, start, 0, unroll=4)
    lax.fori_loop(0, tb * TOP_K, wait, 0, unroll=8)


def dispatch(u, slot, zero_start, zero_count, n_used, n_blocks, tm):
    n, w = u.shape
    tb = ROW_TILE
    n_exp = zero_start.shape[0]
    grid_spec = pltpu.PrefetchScalarGridSpec(
        num_scalar_prefetch=3,
        grid=(n // tb,),
        in_specs=[
            pl.BlockSpec((1, 1, tb * TOP_K), lambda i, zs, zc, nu: (i, 0, 0), memory_space=pltpu.SMEM),
            pl.BlockSpec((tb, w), lambda i, zs, zc, nu: (i, 0)),
        ],
        out_specs=pl.BlockSpec(memory_space=pl.ANY),
        scratch_shapes=[pltpu.VMEM((tm, w), jnp.uint32), pltpu.SemaphoreType.DMA],
    )
    return pl.pallas_call(
        functools.partial(_dispatch_kernel, n_exp=n_exp, tm=tm),
        grid_spec=grid_spec,
        out_shape=jax.ShapeDtypeStruct((n_blocks * tm, w), jnp.uint32),
        compiler_params=_cparams(("arbitrary",)),
        name="dispatch",
    )(zero_start, zero_count, n_used, slot.reshape(n // tb, 1, tb * TOP_K), u)


def _combine_kernel(slot_ref, rg_ref, xl_ref, xc_ref, g_ref, ys_ref, ol_ref, oc_ref, buf_ref, sem, *, n_ctx):
    j = pl.program_id(1)
    tb = rg_ref.shape[0]

    def start(r, c):
        for k in range(TOP_K):
            pltpu.make_async_copy(ys_ref.at[pl.ds(slot_ref[0, 0, r * TOP_K + k], 1)],
                                  buf_ref.at[k, pl.ds(r, 1)], sem).start(priority=k % 2)
        return c

    def wait(r, c):
        pltpu.make_async_copy(ys_ref.at[pl.ds(0, 1)], buf_ref.at[0, pl.ds(0, 1)], sem).wait()
        return c

    lax.fori_loop(0, tb, start, 0, unroll=4)
    lax.fori_loop(0, tb * TOP_K, wait, 0, unroll=8)

    gates = rg_ref[...]
    acc = None
    for k in range(TOP_K):
        ya, yb = _unpack_pair(buf_ref[k])
        term = gates[:, k:k + 1] * jnp.concatenate([ya, yb], axis=1)
        acc = term if acc is None else acc + term

    @pl.when(j < n_ctx)
    def _():
        oc_ref[0] = xc_ref[0] + g_ref[0, 0:1, :] * acc

    @pl.when(j >= n_ctx)
    def _():
        ol_ref[0] = xl_ref[0] + g_ref[0, 1:2, :] * acc


def combine(ys, slot, rg, x_lat, x_ctx, gates):
    b, t, d = x_lat.shape
    ctx = x_ctx.shape[1]
    tb = ROW_TILE
    n_ctx, n_lat = ctx // tb, t // tb
    steps = n_ctx + n_lat
    return pl.pallas_call(
        functools.partial(_combine_kernel, n_ctx=n_ctx),
        grid=(b, steps),
        in_specs=[
            pl.BlockSpec((1, 1, tb * TOP_K), lambda bi, j: (bi * steps + j, 0, 0), memory_space=pltpu.SMEM),
            pl.BlockSpec((tb, LANE), lambda bi, j: (bi * steps + j, 0)),
            pl.BlockSpec((1, tb, d), lambda bi, j: (bi, jnp.maximum(j - n_ctx, 0), 0)),
            pl.BlockSpec((1, tb, d), lambda bi, j: (bi, jnp.minimum(j, n_ctx - 1), 0)),
            pl.BlockSpec((1, 2, d), lambda bi, j: (bi, 0, 0)),
            pl.BlockSpec(memory_space=pl.ANY),
        ],
        out_specs=[pl.BlockSpec((1, tb, d), lambda bi, j: (bi, jnp.maximum(j - n_ctx, 0), 0)),
                   pl.BlockSpec((1, tb, d), lambda bi, j: (bi, jnp.minimum(j, n_ctx - 1), 0))],
        out_shape=[jax.ShapeDtypeStruct(x_lat.shape, f32), jax.ShapeDtypeStruct(x_ctx.shape, f32)],
        scratch_shapes=[pltpu.VMEM((TOP_K, tb, d // 2), jnp.uint32), pltpu.SemaphoreType.DMA],
        compiler_params=_cparams(("arbitrary", "arbitrary")),
        name="combine",
    )(slot.reshape(b * steps, 1, tb * TOP_K), rg, x_lat, x_ctx, gates, ys)


def _final_norm_kernel(x_ref, g_ref, o_ref):
    x = x_ref[0]
    o_ref[0] = (x * lax.rsqrt(jnp.mean(x * x, axis=-1, keepdims=True) + NORM_EPS)) * g_ref[...]


def final_norm(x, g):
    b, t, d = x.shape
    tt = 512 if t % 512 == 0 else ROW_TILE
    return pl.pallas_call(
        _final_norm_kernel,
        grid=(b, t // tt),
        in_specs=[pl.BlockSpec((1, tt, d), lambda bi, j: (bi, j, 0)), pl.BlockSpec((1, d), lambda bi, j: (0, 0))],
        out_specs=pl.BlockSpec((1, tt, d), lambda bi, j: (bi, j, 0)),
        out_shape=jax.ShapeDtypeStruct((b, t, d), f32),
        compiler_params=_cparams(("parallel", "parallel")),
        name="final_norm",
    )(x, g)


MOE_TILE = 512


def _pick_tile(n, prefs):
    for p in prefs:
        if n % p == 0:
            return p
    raise ValueError(f"no tile in {prefs} divides {n}")


def _pad_lanes(w):
    return jnp.pad(w, ((0, 0), (0, LANE - w.shape[1])))


def _project(h, w_main):
    b, s, d = h.shape
    m = b * s
    n = w_main.shape[1]
    out = matmul(h.reshape(m, d), w_main, bf16, _pick_tile(m, (1024, 512, 256)), _pick_tile(n, (2048, 1024, 512, 256, 128)))
    return out.reshape(b, s, n)


def _moe_ffn(u, logits, router_b, layer, gu_w, gu_b, down_w, down_b, x_lat, x_ctx, gates):
    n = u.shape[0]
    e = gu_w.shape[1]
    tm = MOE_TILE
    i32 = jnp.int32
    ri, rg, counts = route(logits, _pad_lanes(router_b.astype(f32)[None, :]), e)
    cnt = counts[0, :e].astype(i32)
    padded = (cnt + tm - 1) // tm * tm
    pad_end = jnp.cumsum(padded)
    pad_start = pad_end - padded
    n_blocks = -(-(n * TOP_K) // tm) + e
    top_idx, rank = ri[:, :TOP_K], ri[:, TOP_K:2 * TOP_K]
    hot = top_idx[:, :, None] == jnp.arange(e, dtype=i32)[None, None, :]
    slot = (jnp.sum(jnp.where(hot, pad_start[None, None, :], 0), axis=-1) + rank).reshape(-1)
    block_start = jnp.arange(n_blocks, dtype=i32) * tm
    block_e = jnp.minimum(jnp.sum((block_start[:, None] >= pad_end[None, :]).astype(i32), axis=1), e - 1)
    n_used = (pad_end[-1] // tm).astype(i32).reshape(1)
    xs = dispatch(u, slot, pad_start + cnt, padded - cnt, n_used, n_blocks, tm)
    ys = moe_experts(xs, block_e, n_used, layer, gu_w, gu_b, down_w, down_b, tm)
    return combine(ys, slot, rg, x_lat, x_ctx, gates)


def kernel(x, c, ctx, c_ctx, mod_w, mod_b, norm1_g, norm2_g, final_g, m_in_w, m_conv_w, m_conv_b, m_dt_bias, m_a_log, m_d, m_norm_g, m_out_w, l_in_w, l_conv_w, l_conv_b, l_gate_b, l_norm_g, l_out_w, router_w, router_b, e_gu_w, e_gu_b, e_down_w, e_down_b):
    b, t, d = x.shape
    n_ctx_tok = ctx.shape[1]
    depth = mod_w.shape[0]
    m_inner = m_out_w.shape[1]
    m_conv = m_conv_w.shape[-1]
    l_qkv = l_conv_w.shape[-1]
    l_vtot = l_out_w.shape[1]

    c_rows = jnp.zeros((16, d), f32).at[:b].set(c).at[b].set(c_ctx)
    mod_all = modulation(c_rows, mod_w, mod_b)

    xl, xc = x, ctx
    for i in range(depth):
        j = i // 2
        mod = mod_all[i].reshape(16, 6, d)
        lat, cx = mod[:b], jnp.broadcast_to(mod[b][None], (b, 6, d))

        def mods(k):
            return jnp.stack([cx[:, k], cx[:, k + 1], lat[:, k], lat[:, k + 1]], axis=1)

        def gate(k):
            return jnp.stack([cx[:, k], lat[:, k]], axis=1)

        if i % 2 == 0:
            w_in = m_in_w[j].astype(bf16)
            n_main = m_inner + m_conv
            h, dt_small = normmod(xl, xc, norm1_g[i][None], mods(0), _pad_lanes(w_in[:, n_main:]), False)
            proj = _project(h, w_in[:, :n_main])
            xbc = conv_silu(proj, m_conv_w[j], m_conv_b[j][None], m_inner, 512, n_ctx_tok)
            y = ssd_mixer(xbc, proj, dt_small, m_dt_bias[j], m_a_log[j], m_d[j], m_norm_g[j], n_ctx_tok)
            xl, xc = outproj_residual(y, m_out_w[j].astype(bf16), xl, xc, gate(2), False)
        else:
            w_in = l_in_w[j].astype(bf16)
            n_main = l_qkv + l_vtot
            h, gate_small = normmod(xl, xc, norm1_g[i][None], mods(0), _pad_lanes(w_in[:, n_main:]), True)
            proj = _project(h, w_in[:, :n_main])
            qkv = conv_silu(proj, l_conv_w[j], l_conv_b[j][None], 0, 512, n_ctx_tok)
            y = mlstm_mixer(qkv, proj, l_qkv, gate_small, l_gate_b[j], l_norm_g[j], n_ctx_tok)
            xl, xc = outproj_residual(y, l_out_w[j].astype(bf16), xl, xc, gate(2), True)

        u, logits = normmod(xl, xc, norm2_g[i][None], mods(3), _pad_lanes(router_w[i].astype(bf16)), False, pack=True)
        s = n_ctx_tok + t
        xl, xc = _moe_ffn(u.reshape(b * s, d // 2), logits.reshape(b * s, LANE), router_b[i], i, e_gu_w, e_gu_b,
                          e_down_w, e_down_b, xl, xc, gate(5))
    return final_norm(xl, final_g[None])
```
